```python
import math
import jax, jax.numpy as jnp
from jax import lax
import numpy as np

D_MODEL = 1024
BATCH = 16
SEQ = 4096
DEPTH = 4

MEM_LEN = 256
EPS = 1e-6
BRANCH_WIDTH = 512
N_BRANCH = 3

A_WIDTH = BRANCH_WIDTH
A_HEADS = 8
A_HEAD_DIM = A_WIDTH // A_HEADS
A_CONV = 4
RG_C = 8.0

B_HEADS = 4
B_DK = BRANCH_WIDTH // B_HEADS
B_DV = BRANCH_WIDTH // B_HEADS
B_CONV = 4
B_CHUNK = 64

C_WIDTH = BRANCH_WIDTH
C_GROUP = 16
C_GROUPS = C_WIDTH // C_GROUP
C_STATE = 64

X_HEADS = 4
X_HEAD_DIM = D_MODEL // X_HEADS

D_FF = 3 * D_MODEL
FFN_CONV = 3

IN_SPLITS = (A_WIDTH, A_WIDTH,
             B_HEADS * B_DK, B_HEADS * B_DK,
             B_HEADS * B_DV, B_HEADS * B_DV,
             B_HEADS, B_HEADS,
             C_WIDTH,
             N_BRANCH * D_MODEL)
D_IN = sum(IN_SPLITS)

kernel_name = "hybrid_rglru_deltanet_s5_block"


def rmsnorm(x, g):
    xf = x.astype(jnp.float32)
    var = jnp.mean(xf * xf, axis=-1, keepdims=True)
    return (xf * lax.rsqrt(var + EPS) * g.astype(jnp.float32)).astype(x.dtype)


def l2norm(x):
    return x * lax.rsqrt(jnp.sum(x * x, axis=-1, keepdims=True) + EPS)


def causal_dwconv(x, w):
    k, ch = w.shape
    return lax.conv_general_dilated(x, w[:, None, :].astype(x.dtype), (1,), [(k - 1, 0)],
                                    dimension_numbers=('NWC', 'WIO', 'NWC'),
                                    feature_group_count=ch)


def _linear_combine(e1, e2):
    a1, b1 = e1
    a2, b2 = e2
    return a1 * a2, a2 * b1 + b2


def _complex_linear_combine(e1, e2):
    ar1, ai1, br1, bi1 = e1
    ar2, ai2, br2, bi2 = e2
    return (ar1 * ar2 - ai1 * ai2,
            ar1 * ai2 + ai1 * ar2,
            ar2 * br1 - ai2 * bi1 + br2,
            ar2 * bi1 + ai2 * br1 + bi2)


def rg_lru(x, w_r, b_r, w_i, b_i, lam):
    bsz, s, _ = x.shape
    xf = x.astype(jnp.float32)
    xh = xf.reshape(bsz, s, A_HEADS, A_HEAD_DIM)
    r = jax.nn.sigmoid(jnp.einsum('bshi,hij->bshj', xh, w_r.astype(jnp.float32)).reshape(bsz, s, A_WIDTH) + b_r.astype(jnp.float32))
    ig = jax.nn.sigmoid(jnp.einsum('bshi,hij->bshj', xh, w_i.astype(jnp.float32)).reshape(bsz, s, A_WIDTH) + b_i.astype(jnp.float32))
    log_a = -RG_C * r * jax.nn.softplus(-lam.astype(jnp.float32))
    a = jnp.exp(log_a)
    first = (jnp.arange(s) == 0)[None, :, None]
    mult = jnp.where(first, 1.0, jnp.sqrt(-jnp.expm1(2.0 * log_a)))
    b = mult * ig * xf
    _, h = lax.associative_scan(_linear_combine, (a, b), axis=1)
    return h.astype(x.dtype)


def chunk_gated_delta_rule(q, k, v, g, beta):
    bsz, s, h, dk = q.shape
    dv = v.shape[-1]
    c = B_CHUNK
    n = s // c

    def chunks(t):
        return t.reshape(bsz, n, c, h, t.shape[-1]).transpose(0, 3, 1, 2, 4)

    qc = chunks(q * dk ** -0.5)
    kc = chunks(k)
    vc = chunks(v)
    gc = jnp.cumsum(g.reshape(bsz, n, c, h).transpose(0, 3, 1, 2), axis=-1)
    bc = beta.reshape(bsz, n, c, h).transpose(0, 3, 1, 2)
    incl = jnp.tril(jnp.ones((c, c), dtype=bool))
    strict = jnp.tril(jnp.ones((c, c), dtype=bool), k=-1)
    decay = jnp.exp(jnp.where(incl, gc[..., :, None] - gc[..., None, :], -jnp.inf))
    kb = kc * bc[..., None]
    a_mat = jnp.where(strict, jnp.einsum('bhnik,bhnjk->bhnij', kb, kc) * decay, 0.0)
    lower = a_mat + jnp.eye(c, dtype=a_mat.dtype)
    rhs = jnp.concatenate([vc * bc[..., None], kb * jnp.exp(gc)[..., None]], axis=-1)
    sol = lax.linalg.triangular_solve(lower, rhs, left_side=True, lower=True, unit_diagonal=True)
    u, w = sol[..., :dv], sol[..., dv:]
    qk = jnp.where(incl, jnp.einsum('bhnik,bhnjk->bhnij', qc, kc) * decay, 0.0)
    q_dec = qc * jnp.exp(gc)[..., None]
    k_dec = kc * jnp.exp(gc[..., -1:] - gc)[..., None]
    g_tot = jnp.exp(gc[..., -1])
    xs = tuple(jnp.moveaxis(t, 2, 0) for t in (u, w, qk, q_dec, k_dec, g_tot))

    def step(state, inp):
        u_n, w_n, qk_n, qd_n, kd_n, gt_n = inp
        v_new = u_n - jnp.einsum('bhck,bhkv->bhcv', w_n, state)
        o = jnp.einsum('bhck,bhkv->bhcv', qd_n, state) + jnp.einsum('bhij,bhjv->bhiv', qk_n, v_new)
        state = state * gt_n[..., None, None] + jnp.einsum('bhck,bhcv->bhkv', kd_n, v_new)
        return state, o

    s0 = jnp.zeros((bsz, h, dk, dv), jnp.float32)
    _, o = lax.scan(step, s0, xs)
    return o.transpose(1, 0, 3, 2, 4).reshape(bsz, s, h, dv)


def s5_layer(u, lam_re, lam_im, log_dt, b_re, b_im, c_re, c_im, d):
    bsz, s, _ = u.shape
    f32 = jnp.float32
    uf = u.astype(f32)
    ug = uf.reshape(bsz, s, C_GROUPS, C_GROUP)
    lr, li = lam_re.astype(f32), lam_im.astype(f32)
    dt = jnp.exp(log_dt.astype(f32))[:, None]
    mag = jnp.exp(lr * dt)
    ar, ai = mag * jnp.cos(li * dt), mag * jnp.sin(li * dt)
    den = lr * lr + li * li
    fr = ((ar - 1.0) * lr + ai * li) / den
    fi = (ai * lr - (ar - 1.0) * li) / den
    br, bi = b_re.astype(f32), b_im.astype(f32)
    bbr = fr[..., None] * br - fi[..., None] * bi
    bbi = fr[..., None] * bi + fi[..., None] * br
    bu_r = jnp.einsum('bsgc,gpc->bsgp', ug, bbr)
    bu_i = jnp.einsum('bsgc,gpc->bsgp', ug, bbi)
    a_r = jnp.broadcast_to(ar, (1, s, C_GROUPS, C_STATE))
    a_i = jnp.broadcast_to(ai, (1, s, C_GROUPS, C_STATE))
    _, _, hr, hi = lax.associative_scan(_complex_linear_combine, (a_r, a_i, bu_r, bu_i), axis=1)
    y = (jnp.einsum('bsgp,gcp->bsgc', hr, c_re.astype(f32))
         - jnp.einsum('bsgp,gcp->bsgc', hi, c_im.astype(f32))).reshape(bsz, s, C_WIDTH)
    return y + d.astype(f32) * uf


def hybrid_mixer(h, w_in, b_gate, a_conv_w, a_conv_b, a_w_r, a_b_r, a_w_i, a_b_i, a_lam,
                 b_conv_w, b_a_log, b_dt_bias, b_norm,
                 c_lam_re, c_lam_im, c_log_dt, c_b_re, c_b_im, c_c_re, c_c_im, c_d, c_glu_w, c_glu_b,
                 w_branch, w_out):
    bsz, s, _ = h.shape
    f32 = jnp.float32
    offs = np.cumsum(IN_SPLITS)[:-1].tolist()
    xa, ga, q, k, v, z, beta_raw, alpha_raw, uc, gates = jnp.split(h @ w_in, offs, axis=-1)

    ya = rg_lru(causal_dwconv(xa, a_conv_w) + a_conv_b, a_w_r, a_b_r, a_w_i, a_b_i, a_lam) * jax.nn.gelu(ga)

    qkv = jax.nn.silu(causal_dwconv(jnp.concatenate([q, k, v], axis=-1), b_conv_w)).astype(f32)
    q, k, v = jnp.split(qkv, [B_HEADS * B_DK, 2 * B_HEADS * B_DK], axis=-1)
    q = l2norm(q.reshape(bsz, s, B_HEADS, B_DK))
    k = l2norm(k.reshape(bsz, s, B_HEADS, B_DK))
    v = v.reshape(bsz, s, B_HEADS, B_DV)
    beta = jax.nn.sigmoid(beta_raw.astype(f32))
    g = -jnp.exp(b_a_log.astype(f32)) * jax.nn.softplus(alpha_raw.astype(f32) + b_dt_bias.astype(f32))
    o = chunk_gated_delta_rule(q, k, v, g, beta)
    o = rmsnorm(o, b_norm) * jax.nn.silu(z.astype(f32).reshape(bsz, s, B_HEADS, B_DV))
    yb = o.reshape(bsz, s, B_HEADS * B_DV).astype(h.dtype)

    yc = jax.nn.gelu(s5_layer(uc, c_lam_re, c_lam_im, c_log_dt, c_b_re, c_b_im, c_c_re, c_c_im, c_d))
    yc = (yc * jax.nn.sigmoid(yc @ c_glu_w.astype(f32) + c_glu_b.astype(f32))).astype(h.dtype)

    branches = jnp.stack([ya, yb, yc], axis=2)
    proj = jnp.einsum('bskc,kcd->bskd', branches, w_branch)
    gate = jax.nn.sigmoid((gates + b_gate).reshape(bsz, s, N_BRANCH, D_MODEL))
    return jnp.sum(gate * proj, axis=2) @ w_out


def cross_attention(h, mem_n, w_q, w_kv, w_o):
    bsz, s, _ = h.shape
    m = mem_n.shape[1]
    q = (h @ w_q).reshape(bsz, s, X_HEADS, X_HEAD_DIM)
    k, v = jnp.split(mem_n @ w_kv, 2, axis=-1)
    k = k.reshape(bsz, m, X_HEADS, X_HEAD_DIM)
    v = v.reshape(bsz, m, X_HEADS, X_HEAD_DIM)
    sc = jnp.einsum('bshd,bmhd->bhsm', q, k).astype(jnp.float32) * (X_HEAD_DIM ** -0.5)
    p = jax.nn.softmax(sc, axis=-1).astype(v.dtype)
    o = jnp.einsum('bhsm,bmhd->bshd', p, v).reshape(bsz, s, D_MODEL)
    return o @ w_o


def conv_ffn(h, w_up, conv_w, conv_b, w_down):
    u = causal_dwconv(h @ w_up, conv_w) + conv_b
    gate, val = jnp.split(u, 2, axis=-1)
    return (jax.nn.gelu(gate) * val) @ w_down


def setup_inputs(seed: int = 0) -> dict:
    key = jax.random.key(seed)
    ks = iter(jax.random.split(key, 64))
    L = DEPTH
    f32 = jnp.float32

    def nrm(shape, scale):
        return jax.random.normal(next(ks), shape, f32) * scale

    def gain(shape):
        return 1.0 + 0.02 * jax.random.normal(next(ks), shape, f32)

    def unif(shape, lo, hi):
        return jax.random.uniform(next(ks), shape, f32, minval=lo, maxval=hi)

    x = nrm((BATCH, SEQ, D_MODEL), 1.0)
    mem = nrm((BATCH, MEM_LEN, D_MODEL), 1.0)
    mix_norm = gain((L, D_MODEL))
    w_in = nrm((L, D_MODEL, D_IN), D_MODEL ** -0.5)
    b_gate = nrm((L, N_BRANCH * D_MODEL), 0.01)
    a_conv_w = nrm((L, A_CONV, A_WIDTH), A_CONV ** -0.5)
    a_conv_b = nrm((L, A_WIDTH), 0.01)
    a_w_r = nrm((L, A_HEADS, A_HEAD_DIM, A_HEAD_DIM), A_HEAD_DIM ** -0.5)
    a_b_r = nrm((L, A_WIDTH), 0.01)
    a_w_i = nrm((L, A_HEADS, A_HEAD_DIM, A_HEAD_DIM), A_HEAD_DIM ** -0.5)
    a_b_i = nrm((L, A_WIDTH), 0.01)
    a_pow = unif((L, A_WIDTH), 0.9, 0.999) ** (1.0 / RG_C)
    a_lam = jnp.log(a_pow) - jnp.log1p(-a_pow)
    b_conv_w = nrm((L, B_CONV, 2 * B_HEADS * B_DK + B_HEADS * B_DV), B_CONV ** -0.5)
    b_a_log = jnp.log(unif((L, B_HEADS), 1.0, 16.0))
    dt = jnp.exp(unif((L, B_HEADS), math.log(0.001), math.log(0.1)))
    b_dt_bias = dt + jnp.log(-jnp.expm1(-dt))
    b_norm = gain((L, B_DV))
    c_lam_re = -0.5 + 0.01 * jax.random.normal(next(ks), (L, C_GROUPS, C_STATE), f32)
    c_lam_im = math.pi * jnp.arange(C_STATE, dtype=f32) + 0.01 * jax.random.normal(next(ks), (L, C_GROUPS, C_STATE), f32)
    c_log_dt = unif((L, C_GROUPS), math.log(0.001), math.log(0.1))
    c_b_re = nrm((L, C_GROUPS, C_STATE, C_GROUP), (2 * C_GROUP) ** -0.5)
    c_b_im = nrm((L, C_GROUPS, C_STATE, C_GROUP), (2 * C_GROUP) ** -0.5)
    c_c_re = nrm((L, C_GROUPS, C_GROUP, C_STATE), C_STATE ** -0.5)
    c_c_im = nrm((L, C_GROUPS, C_GROUP, C_STATE), C_STATE ** -0.5)
    c_d = nrm((L, C_WIDTH), 1.0)
    c_glu_w = nrm((L, C_WIDTH, C_WIDTH), C_WIDTH ** -0.5)
    c_glu_b = nrm((L, C_WIDTH), 0.01)
    w_branch = nrm((L, N_BRANCH, BRANCH_WIDTH, D_MODEL), BRANCH_WIDTH ** -0.5)
    w_out = nrm((L, D_MODEL, D_MODEL), D_MODEL ** -0.5)
    xa_norm = gain((L, D_MODEL))
    mem_norm = gain((L, D_MODEL))
    xa_w_q = nrm((L, D_MODEL, D_MODEL), D_MODEL ** -0.5)
    xa_w_kv = nrm((L, D_MODEL, 2 * D_MODEL), D_MODEL ** -0.5)
    xa_w_o = nrm((L, D_MODEL, D_MODEL), D_MODEL ** -0.5)
    ffn_norm = gain((L, D_MODEL))
    ffn_w_up = nrm((L, D_MODEL, 2 * D_FF), D_MODEL ** -0.5)
    ffn_conv_w = nrm((L, FFN_CONV, 2 * D_FF), FFN_CONV ** -0.5)
    ffn_conv_b = nrm((L, 2 * D_FF), 0.01)
    ffn_w_down = nrm((L, D_FF, D_MODEL), D_FF ** -0.5)
    final_norm = gain((D_MODEL,))
    return {"x": x, "mem": mem, "mix_norm": mix_norm, "w_in": w_in, "b_gate": b_gate,
            "a_conv_w": a_conv_w, "a_conv_b": a_conv_b, "a_w_r": a_w_r, "a_b_r": a_b_r,
            "a_w_i": a_w_i, "a_b_i": a_b_i, "a_lam": a_lam,
            "b_conv_w": b_conv_w, "b_a_log": b_a_log, "b_dt_bias": b_dt_bias, "b_norm": b_norm,
            "c_lam_re": c_lam_re, "c_lam_im": c_lam_im, "c_log_dt": c_log_dt,
            "c_b_re": c_b_re, "c_b_im": c_b_im, "c_c_re": c_c_re, "c_c_im": c_c_im, "c_d": c_d,
            "c_glu_w": c_glu_w, "c_glu_b": c_glu_b, "w_branch": w_branch, "w_out": w_out,
            "xa_norm": xa_norm, "mem_norm": mem_norm, "xa_w_q": xa_w_q, "xa_w_kv": xa_w_kv, "xa_w_o": xa_w_o,
            "ffn_norm": ffn_norm, "ffn_w_up": ffn_w_up, "ffn_conv_w": ffn_conv_w, "ffn_conv_b": ffn_conv_b,
            "ffn_w_down": ffn_w_down, "final_norm": final_norm}


def reference(x, mem, mix_norm, w_in, b_gate, a_conv_w, a_conv_b, a_w_r, a_b_r, a_w_i, a_b_i, a_lam,
              b_conv_w, b_a_log, b_dt_bias, b_norm, c_lam_re, c_lam_im, c_log_dt, c_b_re, c_b_im,
              c_c_re, c_c_im, c_d, c_glu_w, c_glu_b, w_branch, w_out, xa_norm, mem_norm, xa_w_q,
              xa_w_kv, xa_w_o, ffn_norm, ffn_w_up, ffn_conv_w, ffn_conv_b, ffn_w_down, final_norm):
    for l in range(DEPTH):
        h = rmsnorm(x, mix_norm[l])
        x = x + hybrid_mixer(h, w_in[l], b_gate[l], a_conv_w[l], a_conv_b[l], a_w_r[l], a_b_r[l],
                             a_w_i[l], a_b_i[l], a_lam[l], b_conv_w[l], b_a_log[l], b_dt_bias[l], b_norm[l],
                             c_lam_re[l], c_lam_im[l], c_log_dt[l], c_b_re[l], c_b_im[l], c_c_re[l],
                             c_c_im[l], c_d[l], c_glu_w[l], c_glu_b[l], w_branch[l], w_out[l])
        h = rmsnorm(x, xa_norm[l])
        x = x + cross_attention(h, rmsnorm(mem, mem_norm[l]), xa_w_q[l], xa_w_kv[l], xa_w_o[l])
        h = rmsnorm(x, ffn_norm[l])
        x = x + conv_ffn(h, ffn_w_up[l], ffn_conv_w[l], ffn_conv_b[l], ffn_w_down[l])
    return rmsnorm(x, final_norm)
```

```python
import functools
import math

import jax
import jax.numpy as jnp
from jax import lax
from jax.experimental import pallas as pl
from jax.experimental.pallas import tpu as pltpu

F32 = jnp.float32
BF16 = jnp.bfloat16

EPS = 1e-6
D_MODEL = 1024
BRANCH_WIDTH = 512
N_BRANCH = 3
A_HEADS = 8
A_HEAD_DIM = BRANCH_WIDTH // A_HEADS
A_CONV = 4
RG_C = 8.0
B_HEADS = 4
B_DK = 128
B_DV = 128
B_CONV = 4
DELTA_CHUNK = 64
C_GROUP = 16
C_GROUPS = BRANCH_WIDTH // C_GROUP
C_STATE = 64
S5_CHUNK = 16
X_HEADS = 4
X_HEAD_DIM = D_MODEL // X_HEADS
D_FF = 3 * D_MODEL
FFN_CONV = 3
FF_TILE = 512

SUBLANES = 8
VMEM_LIMIT = 56 * 1024 * 1024

HIGHEST = lax.Precision.HIGHEST


def _cparams(n_axes):
    return pltpu.CompilerParams(dimension_semantics=("arbitrary",) * n_axes,
                                vmem_limit_bytes=VMEM_LIMIT)


def _full(shape):
    n = len(shape)
    return pl.BlockSpec(shape, lambda *_: (0,) * n)


def _rms(x, g):
    var = jnp.mean(x * x, axis=-1, keepdims=True)
    return x * lax.rsqrt(var + EPS) * g


def _softplus(x):
    return jnp.maximum(x, 0.0) + jnp.log1p(jnp.exp(-jnp.abs(x)))


def _dot(a, b):
    return jnp.dot(a, b, preferred_element_type=F32)


def _dot_nt(a, b):
    return lax.dot_general(a, b, (((1,), (1,)), ((), ())), preferred_element_type=F32)


def _dot_tn(a, b):
    return lax.dot_general(a, b, (((0,), (0,)), ((), ())), preferred_element_type=F32)


def _dot_hi(a, b):
    return jnp.dot(a, b, preferred_element_type=F32, precision=HIGHEST)


def _shift_rows(x, d, fill):
    row = lax.broadcasted_iota(jnp.int32, x.shape, 0)
    return jnp.where(row >= d, pltpu.roll(x, d, axis=0), fill)


def _mixer_a_kernel(x_ref, g_ref, w_ref, cw_ref, cb_ref, wg_ref, bg_ref, lam_ref, o_ref,
                    pbuf, hstate, *, ts):
    s = pl.program_id(1)

    @pl.when(s == 0)
    def _():
        pbuf[0:SUBLANES, :] = jnp.zeros((SUBLANES, BRANCH_WIDTH), F32)
        hstate[...] = jnp.zeros_like(hstate)

    h = _rms(x_ref[...], g_ref[...]).astype(BF16)
    p = _dot(h, w_ref[...])
    xa = p[:, :BRANCH_WIDTH]
    ga = p[:, BRANCH_WIDTH:]
    pbuf[SUBLANES:SUBLANES + ts, :] = xa
    cw = cw_ref[...]
    xc = cb_ref[...] + cw[0:1, :] * pbuf[SUBLANES - 3:SUBLANES - 3 + ts, :]
    for i in range(1, A_CONV):
        off = SUBLANES - (A_CONV - 1) + i
        xc = xc + cw[i:i + 1, :] * pbuf[off:off + ts, :]
    pbuf[0:SUBLANES, :] = xa[ts - SUBLANES:, :]

    rg = _dot(xc.astype(BF16), wg_ref[...]) + bg_ref[...]
    r = jax.nn.sigmoid(rg[:, :BRANCH_WIDTH])
    ig = jax.nn.sigmoid(rg[:, BRANCH_WIDTH:])
    log_a = (-RG_C) * r * _softplus(-lam_ref[...])
    a = jnp.exp(log_a)
    mult = jnp.sqrt(1.0 - a * a)
    row = lax.broadcasted_iota(jnp.int32, (ts, BRANCH_WIDTH), 0)
    mult = jnp.where(jnp.logical_and(row == 0, s == 0), 1.0, mult)
    b = mult * ig * xc

    d = 1
    while d < ts:
        b = a * _shift_rows(b, d, 0.0) + b
        a = a * _shift_rows(a, d, 1.0)
        d *= 2
    hs = b + a * hstate[...]
    hstate[...] = hs[ts - 1:ts, :]
    o_ref[...] = (hs * jax.nn.gelu(ga)).astype(o_ref.dtype)


def _mixer_a(x, g, w, cw, cb, wg, bg, lam, *, ts):
    bsz, seq, _ = x.shape
    kern = functools.partial(_mixer_a_kernel, ts=ts)
    return pl.pallas_call(
        kern,
        out_shape=jax.ShapeDtypeStruct((bsz, seq, BRANCH_WIDTH), BF16),
        grid=(bsz, seq // ts),
        in_specs=[pl.BlockSpec((None, ts, D_MODEL), lambda b, s: (b, s, 0)),
                  _full(g.shape), _full(w.shape), _full(cw.shape), _full(cb.shape),
                  _full(wg.shape), _full(bg.shape), _full(lam.shape)],
        out_specs=pl.BlockSpec((None, ts, BRANCH_WIDTH), lambda b, s: (b, s, 0)),
        scratch_shapes=[pltpu.VMEM((ts + SUBLANES, BRANCH_WIDTH), F32),
                        pltpu.VMEM((1, BRANCH_WIDTH), F32)],
        compiler_params=_cparams(2),
        name="mixer_a",
    )(x, g, w, cw, cb, wg, bg, lam)


def _mixer_b_kernel(x_ref, g_ref, w_ref, cw_ref, nega_ref, dtb_ref, bn_ref, o_ref,
                    pbuf, state, *, ts):
    s = pl.program_id(1)
    c = DELTA_CHUNK
    n_chunks = ts // c
    qkv_w = 3 * B_HEADS * B_DK

    @pl.when(s == 0)
    def _():
        pbuf[0:SUBLANES, :] = jnp.zeros((SUBLANES, qkv_w), F32)
        state[...] = jnp.zeros_like(state)

    h = _rms(x_ref[...], g_ref[...]).astype(BF16)
    p = _dot(h, w_ref[...])
    pre = p[:, :qkv_w]
    z = p[:, qkv_w:qkv_w + B_HEADS * B_DV]
    ba = p[:, qkv_w + B_HEADS * B_DV:]

    pbuf[SUBLANES:SUBLANES + ts, :] = pre
    cw = cw_ref[...]
    acc = cw[0:1, :] * pbuf[SUBLANES - 3:SUBLANES - 3 + ts, :]
    for i in range(1, B_CONV):
        off = SUBLANES - (B_CONV - 1) + i
        acc = acc + cw[i:i + 1, :] * pbuf[off:off + ts, :]
    pbuf[0:SUBLANES, :] = pre[ts - SUBLANES:, :]
    qkv = jax.nn.silu(acc)

    beta = jax.nn.sigmoid(ba)
    glog = nega_ref[...] * _softplus(ba + dtb_ref[...])
    row1 = lax.broadcasted_iota(jnp.int32, (ts, 128), 0)
    pos1 = row1 % c
    gc = glog
    d = 1
    while d < c:
        gc = gc + jnp.where(pos1 >= d, pltpu.roll(gc, d, axis=0), 0.0)
        d *= 2
    gct = gc.T
    egc = jnp.exp(gc)

    ri = lax.broadcasted_iota(jnp.int32, (ts, ts), 0)
    ci = lax.broadcasted_iota(jnp.int32, (ts, ts), 1)
    same = (ri // c) == (ci // c)
    incl = jnp.logical_and(same, ci <= ri)
    strict = jnp.logical_and(same, ci < ri)
    eye = jnp.where(ri == ci, 1.0, 0.0).astype(F32)

    outs = []
    for hd in range(B_HEADS):
        q = qkv[:, hd * B_DK:(hd + 1) * B_DK]
        k = qkv[:, (B_HEADS + hd) * B_DK:(B_HEADS + hd + 1) * B_DK]
        v = qkv[:, (2 * B_HEADS) * B_DK + hd * B_DV:(2 * B_HEADS) * B_DK + (hd + 1) * B_DV]
        q = q * lax.rsqrt(jnp.sum(q * q, axis=-1, keepdims=True) + EPS) * (B_DK ** -0.5)
        k = k * lax.rsqrt(jnp.sum(k * k, axis=-1, keepdims=True) + EPS)
        beta_c = beta[:, hd:hd + 1]
        gc_c = gc[:, B_HEADS + hd:B_HEADS + hd + 1]
        gc_r = gct[B_HEADS + hd:B_HEADS + hd + 1, :]
        egc_c = egc[:, B_HEADS + hd:B_HEADS + hd + 1]
        decay = jnp.exp(jnp.where(incl, gc_c - gc_r, -jnp.inf))
        kb = k * beta_c
        k16 = k.astype(BF16)
        a_mat = jnp.where(strict, _dot_nt(kb.astype(BF16), k16) * decay, 0.0)
        pw = -a_mat
        tinv = eye + pw
        span = 2
        while span < c:
            pw = _dot_hi(pw, pw)
            tinv = tinv + _dot_hi(tinv, pw)
            span *= 2
        rhs = jnp.concatenate([v * beta_c, kb * egc_c], axis=-1)
        sol = _dot_hi(tinv, rhs)
        u = sol[:, :B_DV]
        w = sol[:, B_DV:]
        qk = jnp.where(incl, _dot_nt(q.astype(BF16), k16) * decay, 0.0)
        q_dec = (q * egc_c).astype(BF16)
        w16 = w.astype(BF16)

        st = state[hd]
        o_inter = []
        v_new = []
        for n in range(n_chunks):
            lo = n * c
            g_last = gc_c[lo + c - 1:lo + c, :]
            k_dec = (k[lo:lo + c, :] * jnp.exp(g_last - gc_c[lo:lo + c, :])).astype(BF16)
            st16 = st.astype(BF16)
            vn = u[lo:lo + c, :] - _dot(w16[lo:lo + c, :], st16)
            o_inter.append(_dot(q_dec[lo:lo + c, :], st16))
            st = st * jnp.exp(g_last) + _dot_tn(k_dec, vn.astype(BF16))
            v_new.append(vn)
        state[hd] = st
        v_all = jnp.concatenate(v_new, axis=0)
        o = jnp.concatenate(o_inter, axis=0) + _dot(qk.astype(BF16), v_all.astype(BF16))
        o = _rms(o, bn_ref[...])
        zh = z[:, hd * B_DV:(hd + 1) * B_DV]
        outs.append(o * jax.nn.silu(zh))
    o_ref[...] = jnp.concatenate(outs, axis=-1).astype(o_ref.dtype)


def _mixer_b(x, g, w, cw, nega, dtb, bn, *, ts):
    bsz, seq, _ = x.shape
    kern = functools.partial(_mixer_b_kernel, ts=ts)
    return pl.pallas_call(
        kern,
        out_shape=jax.ShapeDtypeStruct((bsz, seq, B_HEADS * B_DV), BF16),
        grid=(bsz, seq // ts),
        in_specs=[pl.BlockSpec((None, ts, D_MODEL), lambda b, s: (b, s, 0)),
                  _full(g.shape), _full(w.shape), _full(cw.shape), _full(nega.shape),
                  _full(dtb.shape), _full(bn.shape)],
        out_specs=pl.BlockSpec((None, ts, B_HEADS * B_DV), lambda b, s: (b, s, 0)),
        scratch_shapes=[pltpu.VMEM((ts + SUBLANES, 3 * B_HEADS * B_DK), F32),
                        pltpu.VMEM((B_HEADS, B_DK, B_DV), F32)],
        compiler_params=_cparams(2),
        name="mixer_b",
    )(x, g, w, cw, nega, dtb, bn)


def _proj_kernel(x_ref, g_ref, w_ref, o_ref):
    h = _rms(x_ref[...], g_ref[...]).astype(BF16)
    o_ref[...] = _dot(h, w_ref[...]).astype(o_ref.dtype)


def _norm_proj(x2, g, w, *, tm, out_dtype, name):
    rows = x2.shape[0]
    n = w.shape[1]
    return pl.pallas_call(
        _proj_kernel,
        out_shape=jax.ShapeDtypeStruct((rows, n), out_dtype),
        grid=(rows // tm,),
        in_specs=[pl.BlockSpec((tm, D_MODEL), lambda i: (i, 0)), _full(g.shape), _full(w.shape)],
        out_specs=pl.BlockSpec((tm, n), lambda i: (i, 0)),
        compiler_params=_cparams(1),
        name=name,
    )(x2, g, w)


def _s5_kernel(z_ref, kt_ref, bs_ref, cs_ref, a1_ref, a2_ref, d_ref, o_ref, *, rows, nsteps):
    z = z_ref[...]
    y = _dot(z, kt_ref[...])
    hst = _dot(z, bs_ref[...])
    half = C_STATE
    for kk in range(nsteps):
        d = 1 << kk
        if d >= rows:
            break
        sh = _shift_rows(hst, d, 0.0)
        hst = hst + a1_ref[kk:kk + 1, :] * sh + a2_ref[kk:kk + 1, :] * pltpu.roll(sh, half, axis=1)
    hprev = _shift_rows(hst, 1, 0.0)
    y = y + _dot(hprev.astype(BF16), cs_ref[...])
    y = y + d_ref[...] * z.astype(F32)
    o_ref[...] = jax.nn.gelu(y).astype(o_ref.dtype)


def _s5(zl, kt, bs, cs, a1, a2, dt, *, bsz, rows):
    groups = zl.shape[0]
    lanes = S5_CHUNK * C_GROUP
    nsteps = a1.shape[1]
    kern = functools.partial(_s5_kernel, rows=rows, nsteps=nsteps)
    wspec = lambda shape: pl.BlockSpec((None,) + shape, lambda g, b: (g, 0, 0))
    return pl.pallas_call(
        kern,
        out_shape=jax.ShapeDtypeStruct(zl.shape, BF16),
        grid=(groups, bsz),
        in_specs=[pl.BlockSpec((None, rows, lanes), lambda g, b: (g, b, 0)),
                  wspec((lanes, lanes)), wspec((lanes, 2 * C_STATE)), wspec((2 * C_STATE, lanes)),
                  wspec((nsteps, 2 * C_STATE)), wspec((nsteps, 2 * C_STATE)), wspec((1, lanes))],
        out_specs=pl.BlockSpec((None, rows, lanes), lambda g, b: (g, b, 0)),
        compiler_params=_cparams(2),
        name="s5_groups",
    )(zl, kt, bs, cs, a1, a2, dt)


def _s5_tables(lam_re, lam_im, log_dt, b_re, b_im, c_re, c_im, d, nsteps):
    g, p = lam_re.shape
    cg = C_GROUP
    lc = S5_CHUNK
    dt = jnp.exp(log_dt)[:, None]
    mag = jnp.exp(lam_re * dt)
    ar, ai = mag * jnp.cos(lam_im * dt), mag * jnp.sin(lam_im * dt)
    den = lam_re * lam_re + lam_im * lam_im
    fr = ((ar - 1.0) * lam_re + ai * lam_im) / den
    fi = (ai * lam_re - (ar - 1.0) * lam_im) / den
    bbr = fr[..., None] * b_re - fi[..., None] * b_im
    bbi = fr[..., None] * b_im + fi[..., None] * b_re
    prs, pis = [jnp.ones_like(ar)], [jnp.zeros_like(ai)]
    for _ in range(lc):
        r0, i0 = prs[-1], pis[-1]
        prs.append(r0 * ar - i0 * ai)
        pis.append(r0 * ai + i0 * ar)
    pr, pi = jnp.stack(prs), jnp.stack(pis)
    clr = c_re[None] * pr[:, :, None, :] - c_im[None] * pi[:, :, None, :]
    cli = c_re[None] * pi[:, :, None, :] + c_im[None] * pr[:, :, None, :]
    kern = (jnp.einsum('tgop,gpi->tgoi', clr, bbr, precision=HIGHEST)
            - jnp.einsum('tgop,gpi->tgoi', cli, bbi, precision=HIGHEST))
    jj = jnp.arange(lc)
    tau = jj[None, :] - jj[:, None]
    sel = kern[jnp.clip(tau, 0, lc)]
    sel = jnp.where((tau >= 0)[:, :, None, None, None], sel, 0.0)
    kt = sel.transpose(2, 0, 4, 1, 3).reshape(g, lc * cg, lc * cg)
    pw_r, pw_i = pr[lc - 1 - jj], pi[lc - 1 - jj]
    sb_r = pw_r[..., None] * bbr[None] - pw_i[..., None] * bbi[None]
    sb_i = pw_r[..., None] * bbi[None] + pw_i[..., None] * bbr[None]
    bs = jnp.concatenate([sb_r.transpose(1, 0, 3, 2).reshape(g, lc * cg, p),
                          sb_i.transpose(1, 0, 3, 2).reshape(g, lc * cg, p)], axis=-1)
    co_r, co_i = clr[1:lc + 1], cli[1:lc + 1]
    cs = jnp.concatenate([co_r.transpose(1, 3, 0, 2).reshape(g, p, lc * cg),
                          -co_i.transpose(1, 3, 0, 2).reshape(g, p, lc * cg)], axis=1)
    sr, si = [pr[lc]], [pi[lc]]
    for _ in range(nsteps - 1):
        r0, i0 = sr[-1], si[-1]
        sr.append(r0 * r0 - i0 * i0)
        si.append(2.0 * r0 * i0)
    sr, si = jnp.stack(sr, axis=1), jnp.stack(si, axis=1)
    a1 = jnp.concatenate([sr, sr], axis=-1)
    a2 = jnp.concatenate([-si, si], axis=-1)
    dtile = jnp.tile(d.reshape(g, 1, cg), (1, lc, 1)).reshape(g, 1, lc * cg)
    return kt.astype(BF16), bs.astype(BF16), cs.astype(BF16), a1, a2, dtile


def _merge_kernel(x_ref, ya_ref, yb_ref, yc_ref, g_ref, wg_ref, bg_ref, glu_w_ref, glu_b_ref,
                  wbr_ref, wo_ref, o_ref):
    x = x_ref[...]
    h = _rms(x, g_ref[...]).astype(BF16)
    ycg = yc_ref[...]
    yc = (ycg.astype(F32) * jax.nn.sigmoid(_dot(ycg, glu_w_ref[...]) + glu_b_ref[...])).astype(BF16)
    m = None
    for kk, y in enumerate((ya_ref[...], yb_ref[...], yc)):
        lo = kk * D_MODEL
        gate = jax.nn.sigmoid(_dot(h, wg_ref[:, lo:lo + D_MODEL]) + bg_ref[:, lo:lo + D_MODEL])
        term = gate * _dot(y, wbr_ref[kk])
        m = term if m is None else m + term
    o_ref[...] = x + _dot(m.astype(BF16), wo_ref[...])


def _merge(x2, ya, yb, yc, g, wg, bg, glu_w, glu_b, wbr, wo, *, tm):
    rows = x2.shape[0]
    tok = lambda n: pl.BlockSpec((tm, n), lambda i: (i, 0))
    return pl.pallas_call(
        _merge_kernel,
        out_shape=jax.ShapeDtypeStruct(x2.shape, F32),
        grid=(rows // tm,),
        in_specs=[tok(D_MODEL), tok(BRANCH_WIDTH), tok(BRANCH_WIDTH), tok(BRANCH_WIDTH),
                  _full(g.shape), _full(wg.shape), _full(bg.shape), _full(glu_w.shape),
                  _full(glu_b.shape), _full(wbr.shape), _full(wo.shape)],
        out_specs=tok(D_MODEL),
        compiler_params=_cparams(1),
        name="merge",
    )(x2, ya, yb, yc, g, wg, bg, glu_w, glu_b, wbr, wo)


def _xattn_kernel(x_ref, kv_ref, g_ref, wq_ref, wo_ref, o_ref):
    x = x_ref[...]
    h = _rms(x, g_ref[...]).astype(BF16)
    q = _dot(h, wq_ref[...])
    heads = []
    for hd in range(X_HEADS):
        lo = hd * X_HEAD_DIM
        qh = q[:, lo:lo + X_HEAD_DIM].astype(BF16)
        kh = kv_ref[:, lo:lo + X_HEAD_DIM]
        vh = kv_ref[:, D_MODEL + lo:D_MODEL + lo + X_HEAD_DIM]
        sc = _dot_nt(qh, kh) * (X_HEAD_DIM ** -0.5)
        sc = sc - jnp.max(sc, axis=-1, keepdims=True)
        e = jnp.exp(sc)
        pr = e / jnp.sum(e, axis=-1, keepdims=True)
        heads.append(_dot(pr.astype(BF16), vh))
    o = jnp.concatenate(heads, axis=-1).astype(BF16)
    o_ref[...] = x + _dot(o, wo_ref[...])


def _xattn(x, kv, g, wq, wo, *, ts):
    bsz, seq, _ = x.shape
    mem_len = kv.shape[1]
    return pl.pallas_call(
        _xattn_kernel,
        out_shape=jax.ShapeDtypeStruct(x.shape, F32),
        grid=(bsz, seq // ts),
        in_specs=[pl.BlockSpec((None, ts, D_MODEL), lambda b, s: (b, s, 0)),
                  pl.BlockSpec((None, mem_len, 2 * D_MODEL), lambda b, s: (b, 0, 0)),
                  _full(g.shape), _full(wq.shape), _full(wo.shape)],
        out_specs=pl.BlockSpec((None, ts, D_MODEL), lambda b, s: (b, s, 0)),
        compiler_params=_cparams(2),
        name="xattn",
    )(x, kv, g, wq, wo)


def _ffn_kernel(x_ref, g_ref, wup_ref, cw_ref, cb_ref, wdn_ref, fg_ref, o_ref, ubuf, tail,
                *, ts, final_norm):
    s = pl.program_id(1)

    @pl.when(s == 0)
    def _():
        tail[...] = jnp.zeros_like(tail)

    x = x_ref[...]
    h = _rms(x, g_ref[...]).astype(BF16)
    acc = x
    for n in range(D_FF // FF_TILE):
        parts = []
        for half in range(2):
            lo = half * D_FF + n * FF_TILE
            u = _dot(h, wup_ref[:, lo:lo + FF_TILE])
            ubuf[0:SUBLANES, :] = tail[:, lo:lo + FF_TILE]
            ubuf[SUBLANES:SUBLANES + ts, :] = u
            tail[:, lo:lo + FF_TILE] = u[ts - SUBLANES:, :]
            cv = cb_ref[:, lo:lo + FF_TILE]
            for i in range(FFN_CONV):
                off = SUBLANES - (FFN_CONV - 1) + i
                cv = cv + cw_ref[i:i + 1, lo:lo + FF_TILE] * ubuf[off:off + ts, :]
            parts.append(cv)
        act = (jax.nn.gelu(parts[0]) * parts[1]).astype(BF16)
        acc = acc + _dot(act, wdn_ref[n * FF_TILE:(n + 1) * FF_TILE, :])
    if final_norm:
        acc = _rms(acc, fg_ref[...])
    o_ref[...] = acc


def _ffn(x, g, wup, cw, cb, wdn, fg, *, ts, final_norm):
    bsz, seq, _ = x.shape
    kern = functools.partial(_ffn_kernel, ts=ts, final_norm=final_norm)
    return pl.pallas_call(
        kern,
        out_shape=jax.ShapeDtypeStruct(x.shape, F32),
        grid=(bsz, seq // ts),
        in_specs=[pl.BlockSpec((None, ts, D_MODEL), lambda b, s: (b, s, 0)),
                  _full(g.shape), _full(wup.shape), _full(cw.shape), _full(cb.shape),
                  _full(wdn.shape), _full(fg.shape)],
        out_specs=pl.BlockSpec((None, ts, D_MODEL), lambda b, s: (b, s, 0)),
        scratch_shapes=[pltpu.VMEM((ts + SUBLANES, FF_TILE), F32),
                        pltpu.VMEM((SUBLANES, 2 * D_FF), F32)],
        compiler_params=_cparams(2),
        name="conv_ffn",
    )(x, g, wup, cw, cb, wdn, fg)


def _tile(n, pref):
    t = min(n, pref)
    while n % t:
        t //= 2
    return t


def _block_diag(w):
    hh, dd, _ = w.shape
    eye = jnp.eye(hh, dtype=w.dtype)
    return (eye[:, None, :, None] * w[:, :, None, :]).reshape(hh * dd, hh * dd)


def _row(v):
    return v.reshape(1, -1).astype(F32)


def _layer(x, mem, p, *, last, final_gain):
    bsz, seq, _ = x.shape
    tokens = bsz * seq
    w_in = p["w_in"]
    offs = [0, 512, 1024, 1536, 2048, 2560, 3072, 3076, 3080, 3592, 6664]
    col = lambda i: w_in[:, offs[i]:offs[i + 1]]
    g_mix = _row(p["mix_norm"])

    w_a = jnp.concatenate([col(0), col(1)], axis=1).astype(BF16)
    w_gate = jnp.concatenate([_block_diag(p["a_w_r"]), _block_diag(p["a_w_i"])], axis=1).astype(BF16)
    b_gate_a = jnp.concatenate([p["a_b_r"], p["a_b_i"]]).reshape(1, -1)
    ya = _mixer_a(x, g_mix, w_a, p["a_conv_w"], _row(p["a_conv_b"]), w_gate, b_gate_a,
                  _row(p["a_lam"]), ts=_tile(seq, 256))

    pad = jnp.zeros((D_MODEL, 128 - 2 * B_HEADS), F32)
    w_b = jnp.concatenate([col(2), col(3), col(4), col(5), col(6), col(7), pad], axis=1).astype(BF16)
    lane_pad = lambda v: jnp.concatenate([jnp.zeros((B_HEADS,), F32), v,
                                          jnp.zeros((128 - 2 * B_HEADS,), F32)]).reshape(1, 128)
    yb = _mixer_b(x, g_mix, w_b, p["b_conv_w"], lane_pad(-jnp.exp(p["b_a_log"])),
                  lane_pad(p["b_dt_bias"]), _row(p["b_norm"]), ts=_tile(seq, 256))

    x2 = x.reshape(tokens, D_MODEL)
    uc = _norm_proj(x2, g_mix, col(8).astype(BF16), tm=_tile(tokens, 1024), out_dtype=BF16,
                    name="proj_c")
    rows = seq // S5_CHUNK
    nsteps = max(1, int(math.ceil(math.log2(rows))))
    tabs = _s5_tables(p["c_lam_re"], p["c_lam_im"], p["c_log_dt"], p["c_b_re"], p["c_b_im"],
                      p["c_c_re"], p["c_c_im"], p["c_d"], nsteps)
    zl = uc.reshape(bsz, rows, S5_CHUNK, C_GROUPS, C_GROUP).transpose(3, 0, 1, 2, 4)
    zl = zl.reshape(C_GROUPS, bsz * rows, S5_CHUNK * C_GROUP)
    ycl = _s5(zl, *tabs, bsz=bsz, rows=rows)
    ycg = ycl.reshape(C_GROUPS, bsz, rows, S5_CHUNK, C_GROUP).transpose(1, 2, 3, 0, 4)
    ycg = ycg.reshape(tokens, BRANCH_WIDTH)

    x2 = _merge(x2, ya.reshape(tokens, -1), yb.reshape(tokens, -1), ycg, g_mix,
                col(9).astype(BF16), _row(p["b_gate"]), p["c_glu_w"].astype(BF16), _row(p["c_glu_b"]),
                p["w_branch"].astype(BF16), p["w_out"].astype(BF16), tm=_tile(tokens, 512))
    x = x2.reshape(bsz, seq, D_MODEL)

    mem_len = mem.shape[1]
    kv = _norm_proj(mem.reshape(bsz * mem_len, D_MODEL), _row(p["mem_norm"]),
                    p["xa_w_kv"].astype(BF16), tm=_tile(bsz * mem_len, 512), out_dtype=BF16,
                    name="mem_kv").reshape(bsz, mem_len, 2 * D_MODEL)
    x = _xattn(x, kv, _row(p["xa_norm"]), p["xa_w_q"].astype(BF16), p["xa_w_o"].astype(BF16),
               ts=_tile(seq, 512))

    x = _ffn(x, _row(p["ffn_norm"]), p["ffn_w_up"].astype(BF16), p["ffn_conv_w"],
             _row(p["ffn_conv_b"]), p["ffn_w_down"].astype(BF16), _row(final_gain),
             ts=_tile(seq, 512), final_norm=last)
    return x


_LAYER_KEYS = ("mix_norm", "w_in", "b_gate", "a_conv_w", "a_conv_b", "a_w_r", "a_b_r", "a_w_i", "a_b_i",
               "a_lam", "b_conv_w", "b_a_log", "b_dt_bias", "b_norm", "c_lam_re", "c_lam_im", "c_log_dt",
               "c_b_re", "c_b_im", "c_c_re", "c_c_im", "c_d", "c_glu_w", "c_glu_b", "w_branch", "w_out",
               "xa_norm", "mem_norm", "xa_w_q", "xa_w_kv", "xa_w_o", "ffn_norm", "ffn_w_up", "ffn_conv_w",
               "ffn_conv_b", "ffn_w_down")


@jax.jit
def kernel(x, mem, mix_norm, w_in, b_gate, a_conv_w, a_conv_b, a_w_r, a_b_r, a_w_i, a_b_i, a_lam,
           b_conv_w, b_a_log, b_dt_bias, b_norm, c_lam_re, c_lam_im, c_log_dt, c_b_re, c_b_im,
           c_c_re, c_c_im, c_d, c_glu_w, c_glu_b, w_branch, w_out, xa_norm, mem_norm, xa_w_q,
           xa_w_kv, xa_w_o, ffn_norm, ffn_w_up, ffn_conv_w, ffn_conv_b, ffn_w_down, final_norm):
    stacked = dict(zip(_LAYER_KEYS, (
        mix_norm, w_in, b_gate, a_conv_w, a_conv_b, a_w_r, a_b_r, a_w_i, a_b_i, a_lam,
        b_conv_w, b_a_log, b_dt_bias, b_norm, c_lam_re, c_lam_im, c_log_dt, c_b_re, c_b_im,
        c_c_re, c_c_im, c_d, c_glu_w, c_glu_b, w_branch, w_out, xa_norm, mem_norm, xa_w_q,
        xa_w_kv, xa_w_o, ffn_norm, ffn_w_up, ffn_conv_w, ffn_conv_b, ffn_w_down)))
    depth = w_in.shape[0]
    for l in range(depth):
        p = {k: v[l] for k, v in stacked.items()}
        x = _layer(x, mem, p, last=(l == depth - 1), final_gain=final_norm)
    return x
```

```python
import functools
import math

import jax
import jax.numpy as jnp
from jax import lax
from jax.experimental import pallas as pl
from jax.experimental.pallas import tpu as pltpu

F32 = jnp.float32
BF16 = jnp.bfloat16

EPS = 1e-6
D_MODEL = 1024
BRANCH_WIDTH = 512
N_BRANCH = 3
A_HEADS = 8
A_HEAD_DIM = BRANCH_WIDTH // A_HEADS
A_CONV = 4
RG_C = 8.0
B_HEADS = 4
B_DK = 128
B_DV = 128
B_CONV = 4
DELTA_CHUNK = 64
SOLVE_REFINE_STEPS = 1
C_GROUP = 16
C_GROUPS = BRANCH_WIDTH // C_GROUP
C_STATE = 64
S5_CHUNK = 16
X_HEADS = 4
X_HEAD_DIM = D_MODEL // X_HEADS
D_FF = 3 * D_MODEL
FFN_CONV = 3
FF_TILE = 512

SUBLANES = 8
VMEM_LIMIT = 56 * 1024 * 1024

HIGHEST = lax.Precision.HIGHEST


def _cparams(n_axes):
    return pltpu.CompilerParams(dimension_semantics=("arbitrary",) * n_axes,
                                vmem_limit_bytes=VMEM_LIMIT)


def _full(shape):
    n = len(shape)
    return pl.BlockSpec(shape, lambda *_: (0,) * n)


def _rms(x, g):
    var = jnp.mean(x * x, axis=-1, keepdims=True)
    return x * lax.rsqrt(var + EPS) * g


def _softplus(x):
    return jnp.maximum(x, 0.0) + jnp.log1p(jnp.exp(-jnp.abs(x)))


def _dot(a, b):
    return jnp.dot(a, b, preferred_element_type=F32)


def _dot_nt(a, b):
    return lax.dot_general(a, b, (((1,), (1,)), ((), ())), preferred_element_type=F32)


def _dot_tn(a, b):
    return lax.dot_general(a, b, (((0,), (0,)), ((), ())), preferred_element_type=F32)


def _dot3(a, b):
    ah = a.astype(BF16)
    al = (a - ah.astype(F32)).astype(BF16)
    bh = b.astype(BF16)
    bl = (b - bh.astype(F32)).astype(BF16)
    return _dot(ah, bh) + _dot(ah, bl) + _dot(al, bh)


def _causal_conv(u, prev, w, bias):
    taps = w.shape[0]
    acc = w[taps - 1:taps, :] * u
    if bias is not None:
        acc = acc + bias
    head_src = jnp.concatenate([prev, u[:SUBLANES, :]], axis=0)
    for i in range(taps - 1):
        d = taps - 1 - i
        head = pltpu.roll(head_src, d, axis=0)[SUBLANES:, :]
        shifted = jnp.concatenate([head, pltpu.roll(u, d, axis=0)[SUBLANES:, :]], axis=0)
        acc = acc + w[i:i + 1, :] * shifted
    return acc


def _shift_rows(x, d, fill):
    row = lax.broadcasted_iota(jnp.int32, x.shape, 0)
    return jnp.where(row >= d, pltpu.roll(x, d, axis=0), fill)


def _mixer_a_kernel(x_ref, g_ref, w_ref, cw_ref, cb_ref, wg_ref, bg_ref, lam_ref, o_ref,
                    pbuf, hstate, *, ts):
    s = pl.program_id(1)

    @pl.when(s == 0)
    def _():
        pbuf[...] = jnp.zeros_like(pbuf)
        hstate[...] = jnp.zeros_like(hstate)

    h = _rms(x_ref[...], g_ref[...]).astype(BF16)
    p = _dot(h, w_ref[...])
    xa = p[:, :BRANCH_WIDTH]
    ga = p[:, BRANCH_WIDTH:]
    xc = _causal_conv(xa, pbuf[...], cw_ref[...], cb_ref[...])
    pbuf[...] = xa[ts - SUBLANES:, :]

    rg = _dot(xc.astype(BF16), wg_ref[...]) + bg_ref[...]
    r = jax.nn.sigmoid(rg[:, :BRANCH_WIDTH])
    ig = jax.nn.sigmoid(rg[:, BRANCH_WIDTH:])
    log_a = (-RG_C) * r * _softplus(-lam_ref[...])
    a = jnp.exp(log_a)
    mult = jnp.sqrt(1.0 - a * a)
    row = lax.broadcasted_iota(jnp.int32, (ts, BRANCH_WIDTH), 0)
    mult = jnp.where(jnp.logical_and(row == 0, s == 0), 1.0, mult)
    b = mult * ig * xc

    pos = row % SUBLANES
    d = 1
    while d < SUBLANES:
        inside = pos >= d
        b = a * jnp.where(inside, pltpu.roll(b, d, axis=0), 0.0) + b
        a = a * jnp.where(inside, pltpu.roll(a, d, axis=0), 1.0)
        d *= 2
    carry = hstate[...]
    pieces = []
    for i in range(ts // SUBLANES):
        lo = i * SUBLANES
        hs_i = b[lo:lo + SUBLANES, :] + a[lo:lo + SUBLANES, :] * carry
        pieces.append(hs_i)
        carry = hs_i[SUBLANES - 1:SUBLANES, :]
    hstate[...] = carry
    hs = jnp.concatenate(pieces, axis=0)
    o_ref[...] = (hs * jax.nn.gelu(ga)).astype(o_ref.dtype)


def _mixer_a(x, g, w, cw, cb, wg, bg, lam, *, ts):
    bsz, seq, _ = x.shape
    kern = functools.partial(_mixer_a_kernel, ts=ts)
    return pl.pallas_call(
        kern,
        out_shape=jax.ShapeDtypeStruct((bsz, seq, BRANCH_WIDTH), BF16),
        grid=(bsz, seq // ts),
        in_specs=[pl.BlockSpec((None, ts, D_MODEL), lambda b, s: (b, s, 0)),
                  _full(g.shape), _full(w.shape), _full(cw.shape), _full(cb.shape),
                  _full(wg.shape), _full(bg.shape), _full(lam.shape)],
        out_specs=pl.BlockSpec((None, ts, BRANCH_WIDTH), lambda b, s: (b, s, 0)),
        scratch_shapes=[pltpu.VMEM((SUBLANES, BRANCH_WIDTH), F32),
                        pltpu.VMEM((1, BRANCH_WIDTH), F32)],
        compiler_params=_cparams(2),
        name="mixer_a",
    )(x, g, w, cw, cb, wg, bg, lam)


def _mixer_b_kernel(x_ref, g_ref, w_ref, cw_ref, nega_ref, dtb_ref, bn_ref, o_ref,
                    pbuf, state, *, ts):
    s = pl.program_id(1)
    c = DELTA_CHUNK
    n_chunks = ts // c
    qkv_w = 3 * B_HEADS * B_DK

    @pl.when(s == 0)
    def _():
        pbuf[...] = jnp.zeros_like(pbuf)
        state[...] = jnp.zeros_like(state)

    h = _rms(x_ref[...], g_ref[...]).astype(BF16)
    p = _dot(h, w_ref[...])
    pre = p[:, :qkv_w]
    z = p[:, qkv_w:qkv_w + B_HEADS * B_DV]
    ba = p[:, qkv_w + B_HEADS * B_DV:]

    qkv = jax.nn.silu(_causal_conv(pre, pbuf[...], cw_ref[...], None))
    pbuf[...] = pre[ts - SUBLANES:, :]

    beta = jax.nn.sigmoid(ba)
    glog = nega_ref[...] * _softplus(ba + dtb_ref[...])
    row1 = lax.broadcasted_iota(jnp.int32, (ts, 128), 0)
    pos1 = row1 % c
    gc = glog
    d = 1
    while d < c:
        gc = gc + jnp.where(pos1 >= d, pltpu.roll(gc, d, axis=0), 0.0)
        d *= 2
    gct = gc.T
    egc = jnp.exp(gc)

    ri = lax.broadcasted_iota(jnp.int32, (ts, ts), 0)
    ci = lax.broadcasted_iota(jnp.int32, (ts, ts), 1)
    same = (ri // c) == (ci // c)
    incl = jnp.logical_and(same, ci <= ri)
    strict = jnp.logical_and(same, ci < ri)
    eye = jnp.where(ri == ci, 1.0, 0.0).astype(F32)

    heads = []
    for hd in range(B_HEADS):
        q = qkv[:, hd * B_DK:(hd + 1) * B_DK]
        k = qkv[:, (B_HEADS + hd) * B_DK:(B_HEADS + hd + 1) * B_DK]
        v = qkv[:, (2 * B_HEADS) * B_DK + hd * B_DV:(2 * B_HEADS) * B_DK + (hd + 1) * B_DV]
        q = q * lax.rsqrt(jnp.sum(q * q, axis=-1, keepdims=True) + EPS) * (B_DK ** -0.5)
        k = k * lax.rsqrt(jnp.sum(k * k, axis=-1, keepdims=True) + EPS)
        beta_c = beta[:, hd:hd + 1]
        gc_c = gc[:, B_HEADS + hd:B_HEADS + hd + 1]
        gc_r = gct[B_HEADS + hd:B_HEADS + hd + 1, :]
        egc_c = egc[:, B_HEADS + hd:B_HEADS + hd + 1]
        decay = jnp.exp(jnp.where(incl, gc_c - gc_r, -jnp.inf))
        kb = k * beta_c
        k16 = k.astype(BF16)
        a_mat = jnp.where(strict, _dot_nt(kb.astype(BF16), k16) * decay, 0.0)
        qk = jnp.where(incl, _dot_nt(q.astype(BF16), k16) * decay, 0.0).astype(BF16)
        g_last = [gc_c[n * c + c - 1:n * c + c, :] for n in range(n_chunks)]
        k_dec = [(k[n * c:(n + 1) * c, :] * jnp.exp(g_last[n] - gc_c[n * c:(n + 1) * c, :])).astype(BF16)
                 for n in range(n_chunks)]
        heads.append(dict(a_mat=a_mat, rhs=jnp.concatenate([v * beta_c, kb * egc_c], axis=-1), qk=qk,
                          q_dec=(q * egc_c).astype(BF16), k_dec=k_dec,
                          g_tot=[jnp.exp(gl) for gl in g_last]))
    pw = [(-hv["a_mat"]).astype(BF16) for hv in heads]
    tinv = [eye + p_.astype(F32) for p_ in pw]
    span = 2
    while span < c:
        pw = [_dot(p_, p_).astype(BF16) for p_ in pw]
        tinv = [t_ + _dot(t_.astype(BF16), p_) for t_, p_ in zip(tinv, pw)]
        span *= 2
    t16 = [t_.astype(BF16) for t_ in tinv]
    sol = [_dot(t_, hv["rhs"].astype(BF16)) for t_, hv in zip(t16, heads)]
    for _ in range(SOLVE_REFINE_STEPS):
        resid = [hv["rhs"] - s_ - _dot3(hv["a_mat"], s_) for hv, s_ in zip(heads, sol)]
        sol = [s_ + _dot(t_, r_.astype(BF16)) for s_, t_, r_ in zip(sol, t16, resid)]
    for hv, s_ in zip(heads, sol):
        hv["u"] = s_[:, :B_DV]
        hv["w16"] = s_[:, B_DV:].astype(BF16)

    st = [state[hd] for hd in range(B_HEADS)]
    o_inter = [[] for _ in range(B_HEADS)]
    v_new = [[] for _ in range(B_HEADS)]
    for n in range(n_chunks):
        lo = n * c
        for hd in range(B_HEADS):
            hv = heads[hd]
            st16 = st[hd].astype(BF16)
            vn = hv["u"][lo:lo + c, :] - _dot(hv["w16"][lo:lo + c, :], st16)
            o_inter[hd].append(_dot(hv["q_dec"][lo:lo + c, :], st16))
            st[hd] = st[hd] * hv["g_tot"][n] + _dot_tn(hv["k_dec"][n], vn.astype(BF16))
            v_new[hd].append(vn)

    outs = []
    for hd in range(B_HEADS):
        state[hd] = st[hd]
        v_all = jnp.concatenate(v_new[hd], axis=0).astype(BF16)
        o = jnp.concatenate(o_inter[hd], axis=0) + _dot(heads[hd]["qk"], v_all)
        o = _rms(o, bn_ref[...])
        zh = z[:, hd * B_DV:(hd + 1) * B_DV]
        outs.append(o * jax.nn.silu(zh))
    o_ref[...] = jnp.concatenate(outs, axis=-1).astype(o_ref.dtype)


def _mixer_b(x, g, w, cw, nega, dtb, bn, *, ts):
    bsz, seq, _ = x.shape
    kern = functools.partial(_mixer_b_kernel, ts=ts)
    return pl.pallas_call(
        kern,
        out_shape=jax.ShapeDtypeStruct((bsz, seq, B_HEADS * B_DV), BF16),
        grid=(bsz, seq // ts),
        in_specs=[pl.BlockSpec((None, ts, D_MODEL), lambda b, s: (b, s, 0)),
                  _full(g.shape), _full(w.shape), _full(cw.shape), _full(nega.shape),
                  _full(dtb.shape), _full(bn.shape)],
        out_specs=pl.BlockSpec((None, ts, B_HEADS * B_DV), lambda b, s: (b, s, 0)),
        scratch_shapes=[pltpu.VMEM((SUBLANES, 3 * B_HEADS * B_DK), F32),
                        pltpu.VMEM((B_HEADS, B_DK, B_DV), F32)],
        compiler_params=_cparams(2),
        name="mixer_b",
    )(x, g, w, cw, nega, dtb, bn)


def _proj_kernel(x_ref, g_ref, w_ref, o_ref):
    h = _rms(x_ref[...], g_ref[...]).astype(BF16)
    o_ref[...] = _dot(h, w_ref[...]).astype(o_ref.dtype)


def _norm_proj(x2, g, w, *, tm, out_dtype, name):
    rows = x2.shape[0]
    n = w.shape[1]
    return pl.pallas_call(
        _proj_kernel,
        out_shape=jax.ShapeDtypeStruct((rows, n), out_dtype),
        grid=(rows // tm,),
        in_specs=[pl.BlockSpec((tm, D_MODEL), lambda i: (i, 0)), _full(g.shape), _full(w.shape)],
        out_specs=pl.BlockSpec((tm, n), lambda i: (i, 0)),
        compiler_params=_cparams(1),
        name=name,
    )(x2, g, w)


S5_LANE_GROUPS = 128 // C_GROUP
S5_HALF = S5_CHUNK // 2


def _s5_kernel(u_ref, perm_ref, kt_ref, bs_ref, cs_ref, a1_ref, a2_ref, d_ref, o_ref, nat,
               *, rows, nsteps):
    lanes = 128
    nat[...] = u_ref[...].astype(F32)
    perm = perm_ref[...]
    zb = []
    for jh in range(2):
        vcat = jnp.concatenate(
            [nat[pl.ds(S5_HALF * jh + jl, rows, stride=S5_CHUNK), :] for jl in range(S5_HALF)],
            axis=-1)
        zb.append(_dot(vcat.astype(BF16), perm).astype(BF16))
    yb = ([], [])
    for gl in range(S5_LANE_GROUPS):
        sl = slice(gl * lanes, (gl + 1) * lanes)
        z = jnp.concatenate([zb[0][:, sl], zb[1][:, sl]], axis=-1)
        y = _dot(z, kt_ref[gl])
        hst = _dot(z, bs_ref[gl])
        for kk in range(nsteps):
            d = 1 << kk
            if d >= rows:
                break
            sh = _shift_rows(hst, d, 0.0)
            hst = (hst + a1_ref[gl, kk:kk + 1, :] * sh
                   + a2_ref[gl, kk:kk + 1, :] * pltpu.roll(sh, C_STATE, axis=1))
        hprev = _shift_rows(hst, 1, 0.0)
        y = y + _dot(hprev.astype(BF16), cs_ref[gl])
        y = y + d_ref[gl] * z.astype(F32)
        yg = jax.nn.gelu(y).astype(BF16)
        yb[0].append(yg[:, :lanes])
        yb[1].append(yg[:, lanes:])
    for jh in range(2):
        ynat = _dot(jnp.concatenate(yb[jh], axis=-1), perm)
        for jl in range(S5_HALF):
            nat[pl.ds(S5_HALF * jh + jl, rows, stride=S5_CHUNK), :] = ynat[:, jl * lanes:(jl + 1) * lanes]
    o_ref[...] = nat[...].astype(o_ref.dtype)


def _s5_perm():
    import numpy as np
    n = S5_HALF * S5_LANE_GROUPS * C_GROUP
    idx = np.arange(n).reshape(S5_HALF, S5_LANE_GROUPS, C_GROUP)
    p = np.zeros((n, n), np.float32)
    p[idx.reshape(-1), idx.transpose(1, 0, 2).reshape(-1)] = 1.0
    return jnp.asarray(p, BF16)


def _s5(u, kt, bs, cs, a1, a2, dt):
    bsz, seq, width = u.shape
    rows = seq // S5_CHUNK
    lanes = S5_CHUNK * C_GROUP
    nsteps = a1.shape[1]
    gpb = S5_LANE_GROUPS
    perm = _s5_perm()
    kern = functools.partial(_s5_kernel, rows=rows, nsteps=nsteps)
    wspec = lambda shape: pl.BlockSpec((gpb,) + shape, lambda b, q: (q, 0, 0))
    return pl.pallas_call(
        kern,
        out_shape=jax.ShapeDtypeStruct(u.shape, BF16),
        grid=(bsz, width // 128),
        in_specs=[pl.BlockSpec((None, seq, 128), lambda b, q: (b, 0, q)),
                  _full(perm.shape),
                  wspec((lanes, lanes)), wspec((lanes, 2 * C_STATE)), wspec((2 * C_STATE, lanes)),
                  wspec((nsteps, 2 * C_STATE)), wspec((nsteps, 2 * C_STATE)), wspec((1, lanes))],
        out_specs=pl.BlockSpec((None, seq, 128), lambda b, q: (b, 0, q)),
        scratch_shapes=[pltpu.VMEM((seq, 128), F32)],
        compiler_params=_cparams(2),
        name="s5_groups",
    )(u, perm, kt, bs, cs, a1, a2, dt)


def _s5_tables(lam_re, lam_im, log_dt, b_re, b_im, c_re, c_im, d, nsteps):
    g, p = lam_re.shape
    cg = C_GROUP
    lc = S5_CHUNK
    dt = jnp.exp(log_dt)[:, None]
    mag = jnp.exp(lam_re * dt)
    ar, ai = mag * jnp.cos(lam_im * dt), mag * jnp.sin(lam_im * dt)
    den = lam_re * lam_re + lam_im * lam_im
    fr = ((ar - 1.0) * lam_re + ai * lam_im) / den
    fi = (ai * lam_re - (ar - 1.0) * lam_im) / den
    bbr = fr[..., None] * b_re - fi[..., None] * b_im
    bbi = fr[..., None] * b_im + fi[..., None] * b_re
    prs, pis = [jnp.ones_like(ar)], [jnp.zeros_like(ai)]
    for _ in range(lc):
        r0, i0 = prs[-1], pis[-1]
        prs.append(r0 * ar - i0 * ai)
        pis.append(r0 * ai + i0 * ar)
    pr, pi = jnp.stack(prs), jnp.stack(pis)
    clr = c_re[None] * pr[:, :, None, :] - c_im[None] * pi[:, :, None, :]
    cli = c_re[None] * pi[:, :, None, :] + c_im[None] * pr[:, :, None, :]
    kern = (jnp.einsum('tgop,gpi->tgoi', clr, bbr, precision=HIGHEST)
            - jnp.einsum('tgop,gpi->tgoi', cli, bbi, precision=HIGHEST))
    jj = jnp.arange(lc)
    tau = jj[None, :] - jj[:, None]
    sel = kern[jnp.clip(tau, 0, lc)]
    sel = jnp.where((tau >= 0)[:, :, None, None, None], sel, 0.0)
    kt = sel.transpose(2, 0, 4, 1, 3).reshape(g, lc * cg, lc * cg)
    pw_r, pw_i = pr[lc - 1 - jj], pi[lc - 1 - jj]
    sb_r = pw_r[..., None] * bbr[None] - pw_i[..., None] * bbi[None]
    sb_i = pw_r[..., None] * bbi[None] + pw_i[..., None] * bbr[None]
    bs = jnp.concatenate([sb_r.transpose(1, 0, 3, 2).reshape(g, lc * cg, p),
                          sb_i.transpose(1, 0, 3, 2).reshape(g, lc * cg, p)], axis=-1)
    co_r, co_i = clr[1:lc + 1], cli[1:lc + 1]
    cs = jnp.concatenate([co_r.transpose(1, 3, 0, 2).reshape(g, p, lc * cg),
                          -co_i.transpose(1, 3, 0, 2).reshape(g, p, lc * cg)], axis=1)
    sr, si = [pr[lc]], [pi[lc]]
    for _ in range(nsteps - 1):
        r0, i0 = sr[-1], si[-1]
        sr.append(r0 * r0 - i0 * i0)
        si.append(2.0 * r0 * i0)
    sr, si = jnp.stack(sr, axis=1), jnp.stack(si, axis=1)
    a1 = jnp.concatenate([sr, sr], axis=-1)
    a2 = jnp.concatenate([-si, si], axis=-1)
    dtile = jnp.tile(d.reshape(g, 1, cg), (1, lc, 1)).reshape(g, 1, lc * cg)
    return kt.astype(BF16), bs.astype(BF16), cs.astype(BF16), a1, a2, dtile


def _merge_kernel(x_ref, ya_ref, yb_ref, yc_ref, g_ref, wg_ref, bg_ref, glu_w_ref, glu_b_ref,
                  wbr_ref, wo_ref, o_ref):
    x = x_ref[...]
    h = _rms(x, g_ref[...]).astype(BF16)
    ycg = yc_ref[...]
    yc = (ycg.astype(F32) * jax.nn.sigmoid(_dot(ycg, glu_w_ref[...]) + glu_b_ref[...])).astype(BF16)
    m = None
    for kk, y in enumerate((ya_ref[...], yb_ref[...], yc)):
        lo = kk * D_MODEL
        gate = jax.nn.sigmoid(_dot(h, wg_ref[:, lo:lo + D_MODEL]) + bg_ref[:, lo:lo + D_MODEL])
        term = gate * _dot(y, wbr_ref[kk])
        m = term if m is None else m + term
    o_ref[...] = x + _dot(m.astype(BF16), wo_ref[...])


def _merge(x2, ya, yb, yc, g, wg, bg, glu_w, glu_b, wbr, wo, *, tm):
    rows = x2.shape[0]
    tok = lambda n: pl.BlockSpec((tm, n), lambda i: (i, 0))
    return pl.pallas_call(
        _merge_kernel,
        out_shape=jax.ShapeDtypeStruct(x2.shape, F32),
        grid=(rows // tm,),
        in_specs=[tok(D_MODEL), tok(BRANCH_WIDTH), tok(BRANCH_WIDTH), tok(BRANCH_WIDTH),
                  _full(g.shape), _full(wg.shape), _full(bg.shape), _full(glu_w.shape),
                  _full(glu_b.shape), _full(wbr.shape), _full(wo.shape)],
        out_specs=tok(D_MODEL),
        compiler_params=_cparams(1),
        name="merge",
    )(x2, ya, yb, yc, g, wg, bg, glu_w, glu_b, wbr, wo)


def _xattn_kernel(x_ref, kv_ref, g_ref, wq_ref, wo_ref, o_ref):
    x = x_ref[...]
    h = _rms(x, g_ref[...]).astype(BF16)
    q = _dot(h, wq_ref[...])
    heads = []
    for hd in range(X_HEADS):
        lo = hd * X_HEAD_DIM
        qh = q[:, lo:lo + X_HEAD_DIM].astype(BF16)
        kh = kv_ref[:, lo:lo + X_HEAD_DIM]
        vh = kv_ref[:, D_MODEL + lo:D_MODEL + lo + X_HEAD_DIM]
        sc = _dot_nt(qh, kh) * (X_HEAD_DIM ** -0.5)
        sc = sc - jnp.max(sc, axis=-1, keepdims=True)
        e = jnp.exp(sc)
        pr = e / jnp.sum(e, axis=-1, keepdims=True)
        heads.append(_dot(pr.astype(BF16), vh))
    o = jnp.concatenate(heads, axis=-1).astype(BF16)
    o_ref[...] = x + _dot(o, wo_ref[...])


def _xattn(x, kv, g, wq, wo, *, ts):
    bsz, seq, _ = x.shape
    mem_len = kv.shape[1]
    return pl.pallas_call(
        _xattn_kernel,
        out_shape=jax.ShapeDtypeStruct(x.shape, F32),
        grid=(bsz, seq // ts),
        in_specs=[pl.BlockSpec((None, ts, D_MODEL), lambda b, s: (b, s, 0)),
                  pl.BlockSpec((None, mem_len, 2 * D_MODEL), lambda b, s: (b, 0, 0)),
                  _full(g.shape), _full(wq.shape), _full(wo.shape)],
        out_specs=pl.BlockSpec((None, ts, D_MODEL), lambda b, s: (b, s, 0)),
        compiler_params=_cparams(2),
        name="xattn",
    )(x, kv, g, wq, wo)


def _ffn_kernel(x_ref, g_ref, wup_ref, cw_ref, cb_ref, wdn_ref, fg_ref, o_ref, tail,
                *, ts, final_norm):
    s = pl.program_id(1)

    @pl.when(s == 0)
    def _():
        tail[...] = jnp.zeros_like(tail)

    x = x_ref[...]
    h = _rms(x, g_ref[...]).astype(BF16)
    acc = x
    for n in range(D_FF // FF_TILE):
        parts = []
        for half in range(2):
            lo = half * D_FF + n * FF_TILE
            u = _dot(h, wup_ref[:, lo:lo + FF_TILE])
            parts.append(_causal_conv(u, tail[:, lo:lo + FF_TILE], cw_ref[:, lo:lo + FF_TILE],
                                      cb_ref[:, lo:lo + FF_TILE]))
            tail[:, lo:lo + FF_TILE] = u[ts - SUBLANES:, :]
        act = (jax.nn.gelu(parts[0]) * parts[1]).astype(BF16)
        acc = acc + _dot(act, wdn_ref[n * FF_TILE:(n + 1) * FF_TILE, :])
    if final_norm:
        acc = _rms(acc, fg_ref[...])
    o_ref[...] = acc


def _ffn(x, g, wup, cw, cb, wdn, fg, *, ts, final_norm):
    bsz, seq, _ = x.shape
    kern = functools.partial(_ffn_kernel, ts=ts, final_norm=final_norm)
    return pl.pallas_call(
        kern,
        out_shape=jax.ShapeDtypeStruct(x.shape, F32),
        grid=(bsz, seq // ts),
        in_specs=[pl.BlockSpec((None, ts, D_MODEL), lambda b, s: (b, s, 0)),
                  _full(g.shape), _full(wup.shape), _full(cw.shape), _full(cb.shape),
                  _full(wdn.shape), _full(fg.shape)],
        out_specs=pl.BlockSpec((None, ts, D_MODEL), lambda b, s: (b, s, 0)),
        scratch_shapes=[pltpu.VMEM((SUBLANES, 2 * D_FF), F32)],
        compiler_params=_cparams(2),
        name="conv_ffn",
    )(x, g, wup, cw, cb, wdn, fg)


def _tile(n, pref):
    t = min(n, pref)
    while n % t:
        t //= 2
    return t


def _block_diag(w):
    hh, dd, _ = w.shape
    eye = jnp.eye(hh, dtype=w.dtype)
    return (eye[:, None, :, None] * w[:, :, None, :]).reshape(hh * dd, hh * dd)


def _row(v):
    return v.reshape(1, -1).astype(F32)


def _layer(x, mem, p, *, last, final_gain):
    bsz, seq, _ = x.shape
    tokens = bsz * seq
    w_in = p["w_in"]
    offs = [0, 512, 1024, 1536, 2048, 2560, 3072, 3076, 3080, 3592, 6664]
    col = lambda i: w_in[:, offs[i]:offs[i + 1]]
    g_mix = _row(p["mix_norm"])

    w_a = jnp.concatenate([col(0), col(1)], axis=1).astype(BF16)
    w_gate = jnp.concatenate([_block_diag(p["a_w_r"]), _block_diag(p["a_w_i"])], axis=1).astype(BF16)
    b_gate_a = jnp.concatenate([p["a_b_r"], p["a_b_i"]]).reshape(1, -1)
    ya = _mixer_a(x, g_mix, w_a, p["a_conv_w"], _row(p["a_conv_b"]), w_gate, b_gate_a,
                  _row(p["a_lam"]), ts=_tile(seq, 256))

    pad = jnp.zeros((D_MODEL, 128 - 2 * B_HEADS), F32)
    w_b = jnp.concatenate([col(2), col(3), col(4), col(5), col(6), col(7), pad], axis=1).astype(BF16)
    lane_pad = lambda v: jnp.concatenate([jnp.zeros((B_HEADS,), F32), v,
                                          jnp.zeros((128 - 2 * B_HEADS,), F32)]).reshape(1, 128)
    yb = _mixer_b(x, g_mix, w_b, p["b_conv_w"], lane_pad(-jnp.exp(p["b_a_log"])),
                  lane_pad(p["b_dt_bias"]), _row(p["b_norm"]), ts=_tile(seq, 256))

    x2 = x.reshape(tokens, D_MODEL)
    uc = _norm_proj(x2, g_mix, col(8).astype(BF16), tm=_tile(tokens, 1024), out_dtype=BF16,
                    name="proj_c")
    rows = seq // S5_CHUNK
    nsteps = max(1, int(math.ceil(math.log2(rows))))
    tabs = _s5_tables(p["c_lam_re"], p["c_lam_im"], p["c_log_dt"], p["c_b_re"], p["c_b_im"],
                      p["c_c_re"], p["c_c_im"], p["c_d"], nsteps)
    ycg = _s5(uc.reshape(bsz, seq, BRANCH_WIDTH), *tabs).reshape(tokens, BRANCH_WIDTH)

    x2 = _merge(x2, ya.reshape(tokens, -1), yb.reshape(tokens, -1), ycg, g_mix,
                col(9).astype(BF16), _row(p["b_gate"]), p["c_glu_w"].astype(BF16), _row(p["c_glu_b"]),
                p["w_branch"].astype(BF16), p["w_out"].astype(BF16), tm=_tile(tokens, 512))
    x = x2.reshape(bsz, seq, D_MODEL)

    mem_len = mem.shape[1]
    kv = _norm_proj(mem.reshape(bsz * mem_len, D_MODEL), _row(p["mem_norm"]),
                    p["xa_w_kv"].astype(BF16), tm=_tile(bsz * mem_len, 512), out_dtype=BF16,
                    name="mem_kv").reshape(bsz, mem_len, 2 * D_MODEL)
    x = _xattn(x, kv, _row(p["xa_norm"]), p["xa_w_q"].astype(BF16), p["xa_w_o"].astype(BF16),
               ts=_tile(seq, 512))

    x = _ffn(x, _row(p["ffn_norm"]), p["ffn_w_up"].astype(BF16), p["ffn_conv_w"],
             _row(p["ffn_conv_b"]), p["ffn_w_down"].astype(BF16), _row(final_gain),
             ts=_tile(seq, 512), final_norm=last)
    return x


_LAYER_KEYS = ("mix_norm", "w_in", "b_gate", "a_conv_w", "a_conv_b", "a_w_r", "a_b_r", "a_w_i", "a_b_i",
               "a_lam", "b_conv_w", "b_a_log", "b_dt_bias", "b_norm", "c_lam_re", "c_lam_im", "c_log_dt",
               "c_b_re", "c_b_im", "c_c_re", "c_c_im", "c_d", "c_glu_w", "c_glu_b", "w_branch", "w_out",
               "xa_norm", "mem_norm", "xa_w_q", "xa_w_kv", "xa_w_o", "ffn_norm", "ffn_w_up", "ffn_conv_w",
               "ffn_conv_b", "ffn_w_down")


@jax.jit
def kernel(x, mem, mix_norm, w_in, b_gate, a_conv_w, a_conv_b, a_w_r, a_b_r, a_w_i, a_b_i, a_lam,
           b_conv_w, b_a_log, b_dt_bias, b_norm, c_lam_re, c_lam_im, c_log_dt, c_b_re, c_b_im,
           c_c_re, c_c_im, c_d, c_glu_w, c_glu_b, w_branch, w_out, xa_norm, mem_norm, xa_w_q,
           xa_w_kv, xa_w_o, ffn_norm, ffn_w_up, ffn_conv_w, ffn_conv_b, ffn_w_down, final_norm):
    stacked = dict(zip(_LAYER_KEYS, (
        mix_norm, w_in, b_gate, a_conv_w, a_conv_b, a_w_r, a_b_r, a_w_i, a_b_i, a_lam,
        b_conv_w, b_a_log, b_dt_bias, b_norm, c_lam_re, c_lam_im, c_log_dt, c_b_re, c_b_im,
        c_c_re, c_c_im, c_d, c_glu_w, c_glu_b, w_branch, w_out, xa_norm, mem_norm, xa_w_q,
        xa_w_kv, xa_w_o, ffn_norm, ffn_w_up, ffn_conv_w, ffn_conv_b, ffn_w_down)))
    depth = w_in.shape[0]
    for l in range(depth):
        p = {k: v[l] for k, v in stacked.items()}
        x = _layer(x, mem, p, last=(l == depth - 1), final_gain=final_norm)
    return x
```

```python
import functools
import math

import jax
import jax.numpy as jnp
from jax import lax
from jax.experimental import pallas as pl
from jax.experimental.pallas import tpu as pltpu

F32 = jnp.float32
BF16 = jnp.bfloat16

EPS = 1e-6
D_MODEL = 1024
BRANCH_WIDTH = 512
N_BRANCH = 3
A_HEADS = 8
A_HEAD_DIM = BRANCH_WIDTH // A_HEADS
A_CONV = 4
RG_C = 8.0
B_HEADS = 4
B_DK = 128
B_DV = 128
B_CONV = 4
DELTA_CHUNK = 64
SOLVE_REFINE_STEPS = 1
C_GROUP = 16
C_GROUPS = BRANCH_WIDTH // C_GROUP
C_STATE = 64
S5_CHUNK = 16
X_HEADS = 4
X_HEAD_DIM = D_MODEL // X_HEADS
D_FF = 3 * D_MODEL
FFN_CONV = 3
FF_TILE = 512

SUBLANES = 8
VMEM_LIMIT = 56 * 1024 * 1024

HIGHEST = lax.Precision.HIGHEST


def _cparams(n_axes):
    return pltpu.CompilerParams(dimension_semantics=("arbitrary",) * n_axes,
                                vmem_limit_bytes=VMEM_LIMIT)


def _full(shape):
    n = len(shape)
    return pl.BlockSpec(shape, lambda *_: (0,) * n)


def _rms(x, g):
    var = jnp.mean(x * x, axis=-1, keepdims=True)
    return x * lax.rsqrt(var + EPS) * g


_GELU_C = -2.0 * math.sqrt(2.0 / math.pi) * math.log2(math.e)


def _gelu(x):
    t = (x * x) * (0.044715 * _GELU_C) + _GELU_C
    return x / (1.0 + jnp.exp2(t * x))


def _softplus(x):
    return jnp.maximum(x, 0.0) + jnp.log1p(jnp.exp(-jnp.abs(x)))


def _dot(a, b):
    return jnp.dot(a, b, preferred_element_type=F32)


def _dot_nt(a, b):
    return lax.dot_general(a, b, (((1,), (1,)), ((), ())), preferred_element_type=F32)


def _dot3(a, b):
    ah = a.astype(BF16)
    al = (a - ah.astype(F32)).astype(BF16)
    bh = b.astype(BF16)
    bl = (b - bh.astype(F32)).astype(BF16)
    return _dot(ah, bh) + _dot(ah, bl) + _dot(al, bh)


def _causal_conv(u, prev, w, bias):
    taps = w.shape[0]
    rows, width = u.shape
    acc = w[taps - 1:taps, :] * u
    if bias is not None:
        acc = acc + bias
    u3 = u.reshape(rows // SUBLANES, SUBLANES, width)
    sub = lax.broadcasted_iota(jnp.int32, u3.shape, 1)
    for i in range(taps - 1):
        d = taps - 1 - i
        rot = pltpu.roll(u3, d, axis=1)
        rot_prev = jnp.concatenate([pltpu.roll(prev, d, axis=0)[None], rot[:-1]], axis=0)
        shifted = jnp.where(sub < d, rot_prev, rot).reshape(rows, width)
        acc = acc + w[i:i + 1, :] * shifted
    return acc


def _shift_rows(x, d, fill):
    row = lax.broadcasted_iota(jnp.int32, x.shape, 0)
    return jnp.where(row >= d, pltpu.roll(x, d, axis=0), fill)


def _mixer_a_kernel(x_ref, g_ref, w_ref, cw_ref, cb_ref, wg_ref, bg_ref, lam_ref, o_ref,
                    pbuf, hstate, *, ts):
    s = pl.program_id(1)

    @pl.when(s == 0)
    def _():
        pbuf[...] = jnp.zeros_like(pbuf)
        hstate[...] = jnp.zeros_like(hstate)

    h = _rms(x_ref[...], g_ref[...]).astype(BF16)
    p = _dot(h, w_ref[...])
    xa = p[:, :BRANCH_WIDTH]
    ga = p[:, BRANCH_WIDTH:]
    xc = _causal_conv(xa, pbuf[...], cw_ref[...], cb_ref[...])
    pbuf[...] = xa[ts - SUBLANES:, :]

    rg = _dot(xc.astype(BF16), wg_ref[...]) + bg_ref[...]
    r = jax.nn.sigmoid(rg[:, :BRANCH_WIDTH])
    ig = jax.nn.sigmoid(rg[:, BRANCH_WIDTH:])
    log_a = (-RG_C) * r * _softplus(-lam_ref[...])
    a = jnp.exp(log_a)
    om = 1.0 - a * a
    mult = jnp.where(om > 0.0, om * lax.rsqrt(om), 0.0)
    row = lax.broadcasted_iota(jnp.int32, (ts, BRANCH_WIDTH), 0)
    mult = jnp.where(jnp.logical_and(row == 0, s == 0), 1.0, mult)
    b = mult * ig * xc

    a3 = a.reshape(ts // SUBLANES, SUBLANES, BRANCH_WIDTH)
    b3 = b.reshape(ts // SUBLANES, SUBLANES, BRANCH_WIDTH)
    pos = lax.broadcasted_iota(jnp.int32, a3.shape, 1)
    d = 1
    while d < SUBLANES:
        inside = pos >= d
        b3 = a3 * jnp.where(inside, pltpu.roll(b3, d, axis=1), 0.0) + b3
        a3 = a3 * jnp.where(inside, pltpu.roll(a3, d, axis=1), 1.0)
        d *= 2
    carry = hstate[...]
    pieces = []
    for i in range(ts // SUBLANES):
        hs_i = b3[i] + a3[i] * carry
        pieces.append(hs_i)
        carry = hs_i[SUBLANES - 1:SUBLANES, :]
    hstate[...] = carry
    hs = jnp.concatenate(pieces, axis=0)
    o_ref[...] = (hs * _gelu(ga)).astype(o_ref.dtype)


def _mixer_a(x, g, w, cw, cb, wg, bg, lam, *, ts):
    bsz, seq, _ = x.shape
    kern = functools.partial(_mixer_a_kernel, ts=ts)
    return pl.pallas_call(
        kern,
        out_shape=jax.ShapeDtypeStruct((bsz, seq, BRANCH_WIDTH), BF16),
        grid=(bsz, seq // ts),
        in_specs=[pl.BlockSpec((None, ts, D_MODEL), lambda b, s: (b, s, 0)),
                  _full(g.shape), _full(w.shape), _full(cw.shape), _full(cb.shape),
                  _full(wg.shape), _full(bg.shape), _full(lam.shape)],
        out_specs=pl.BlockSpec((None, ts, BRANCH_WIDTH), lambda b, s: (b, s, 0)),
        scratch_shapes=[pltpu.VMEM((SUBLANES, BRANCH_WIDTH), F32),
                        pltpu.VMEM((1, BRANCH_WIDTH), F32)],
        compiler_params=_cparams(2),
        name="mixer_a",
    )(x, g, w, cw, cb, wg, bg, lam)


def _mixer_b_kernel(x_ref, g_ref, w_ref, cw_ref, nega_ref, dtb_ref, bn_ref, o_ref,
                    pbuf, state, *, ts, nb):
    s = pl.program_id(1)
    c = DELTA_CHUNK
    n_chunks = ts // c
    qkv_w = 3 * B_HEADS * B_DK

    @pl.when(s == 0)
    def _():
        pbuf[...] = jnp.zeros_like(pbuf)
        state[...] = jnp.zeros_like(state)

    ri = lax.broadcasted_iota(jnp.int32, (ts, ts), 0)
    ci = lax.broadcasted_iota(jnp.int32, (ts, ts), 1)
    same = (ri // c) == (ci // c)
    incl = jnp.logical_and(same, ci <= ri)
    strict = jnp.logical_and(same, ci < ri)
    eye = jnp.where(ri == ci, 1.0, 0.0).astype(F32)
    pos1 = lax.broadcasted_iota(jnp.int32, (ts, 128), 0) % c

    heads = []
    zs = []
    for r in range(nb):
        h = _rms(x_ref[r], g_ref[...]).astype(BF16)
        p = _dot(h, w_ref[...])
        pre = p[:, :qkv_w]
        zs.append(p[:, qkv_w:qkv_w + B_HEADS * B_DV])
        ba = p[:, qkv_w + B_HEADS * B_DV:]
        qkv = jax.nn.silu(_causal_conv(pre, pbuf[r], cw_ref[...], None))
        pbuf[r] = pre[ts - SUBLANES:, :]
        beta = jax.nn.sigmoid(ba)
        gc = nega_ref[...] * _softplus(ba + dtb_ref[...])
        d = 1
        while d < c:
            gc = gc + jnp.where(pos1 >= d, pltpu.roll(gc, d, axis=0), 0.0)
            d *= 2
        gct = gc.T
        egc = jnp.exp(gc)
        for hd in range(B_HEADS):
            q = qkv[:, hd * B_DK:(hd + 1) * B_DK]
            k = qkv[:, (B_HEADS + hd) * B_DK:(B_HEADS + hd + 1) * B_DK]
            v = qkv[:, (2 * B_HEADS) * B_DK + hd * B_DV:(2 * B_HEADS) * B_DK + (hd + 1) * B_DV]
            q = q * lax.rsqrt(jnp.sum(q * q, axis=-1, keepdims=True) + EPS) * (B_DK ** -0.5)
            k = k * lax.rsqrt(jnp.sum(k * k, axis=-1, keepdims=True) + EPS)
            beta_c = beta[:, hd:hd + 1]
            gc_c = gc[:, B_HEADS + hd:B_HEADS + hd + 1]
            gc_r = gct[B_HEADS + hd:B_HEADS + hd + 1, :]
            egc_c = egc[:, B_HEADS + hd:B_HEADS + hd + 1]
            decay = jnp.exp(jnp.where(incl, gc_c - gc_r, -jnp.inf))
            kb = k * beta_c
            k16 = k.astype(BF16)
            a_mat = jnp.where(strict, _dot_nt(kb.astype(BF16), k16) * decay, 0.0)
            qk = (_dot_nt(q.astype(BF16), k16) * decay).astype(BF16)
            g_last = [gc_c[n * c + c - 1:n * c + c, :] for n in range(n_chunks)]
            k_dec = [k[n * c:(n + 1) * c, :] * jnp.exp(g_last[n] - gc_c[n * c:(n + 1) * c, :])
                     for n in range(n_chunks)]
            heads.append(dict(a_mat=a_mat, rhs=jnp.concatenate([v * beta_c, kb * egc_c], axis=-1),
                              qk=qk, q_dec=(q * egc_c).astype(BF16), k_dec=k_dec,
                              g_tot=[jnp.exp(gl) for gl in g_last]))
    pw = [(-hv["a_mat"]).astype(BF16) for hv in heads]
    tinv = [eye + p_.astype(F32) for p_ in pw]
    span = 2
    while span < c:
        pw = [_dot(p_, p_).astype(BF16) for p_ in pw]
        tinv = [t_ + _dot(t_.astype(BF16), p_) for t_, p_ in zip(tinv, pw)]
        span *= 2
    t16 = [t_.astype(BF16) for t_ in tinv]
    sol = [_dot(t_, hv["rhs"].astype(BF16)) for t_, hv in zip(t16, heads)]
    for _ in range(SOLVE_REFINE_STEPS):
        resid = [hv["rhs"] - s_ - _dot3(hv["a_mat"], s_) for hv, s_ in zip(heads, sol)]
        sol = [s_ + _dot(t_, r_.astype(BF16)) for s_, t_, r_ in zip(sol, t16, resid)]
    for hv, s_ in zip(heads, sol):
        hv["u"] = s_[:, :B_DV]
        hv["w16"] = s_[:, B_DV:].astype(BF16)

    n_pairs = nb * B_HEADS
    st = [state[i // B_HEADS, i % B_HEADS] for i in range(n_pairs)]
    o_inter = [[] for _ in range(n_pairs)]
    v_new = [[] for _ in range(n_pairs)]
    zero_s = jnp.zeros((B_DK, B_DV), BF16)
    zero_v = jnp.zeros((c, B_DV), BF16)
    for n in range(n_chunks):
        lo = n * c
        for i in range(0, n_pairs, 2):
            h0, h1 = heads[i], heads[i + 1]
            lhs = jnp.concatenate(
                [jnp.concatenate([h0["w16"][lo:lo + c, :], h1["w16"][lo:lo + c, :]], axis=1),
                 jnp.concatenate([h0["q_dec"][lo:lo + c, :], h1["q_dec"][lo:lo + c, :]], axis=1)],
                axis=0)
            s2 = jnp.concatenate(
                [jnp.concatenate([st[i].astype(BF16), zero_s], axis=1),
                 jnp.concatenate([zero_s, st[i + 1].astype(BF16)], axis=1)], axis=0)
            ws = _dot(lhs, s2)
            vn0 = h0["u"][lo:lo + c, :] - ws[:c, :B_DV]
            vn1 = h1["u"][lo:lo + c, :] - ws[:c, B_DV:]
            o_inter[i].append(ws[c:, :B_DV])
            o_inter[i + 1].append(ws[c:, B_DV:])
            kdt = jnp.concatenate([h0["k_dec"][n], h1["k_dec"][n]], axis=0).T.astype(BF16)
            v2 = jnp.concatenate(
                [jnp.concatenate([vn0.astype(BF16), zero_v], axis=1),
                 jnp.concatenate([zero_v, vn1.astype(BF16)], axis=1)], axis=0)
            upd = _dot(kdt, v2)
            st[i] = st[i] * h0["g_tot"][n] + upd[:, :B_DV]
            st[i + 1] = st[i + 1] * h1["g_tot"][n] + upd[:, B_DV:]
            v_new[i].append(vn0)
            v_new[i + 1].append(vn1)

    for r in range(nb):
        outs = []
        for hd in range(B_HEADS):
            i = r * B_HEADS + hd
            state[r, hd] = st[i]
            v_all = jnp.concatenate(v_new[i], axis=0).astype(BF16)
            o = jnp.concatenate(o_inter[i], axis=0) + _dot(heads[i]["qk"], v_all)
            o = _rms(o, bn_ref[...])
            outs.append(o * jax.nn.silu(zs[r][:, hd * B_DV:(hd + 1) * B_DV]))
        o_ref[r] = jnp.concatenate(outs, axis=-1).astype(o_ref.dtype)


def _mixer_b(x, g, w, cw, nega, dtb, bn, *, ts, nb):
    bsz, seq, _ = x.shape
    kern = functools.partial(_mixer_b_kernel, ts=ts, nb=nb)
    return pl.pallas_call(
        kern,
        out_shape=jax.ShapeDtypeStruct((bsz, seq, B_HEADS * B_DV), BF16),
        grid=(bsz // nb, seq // ts),
        in_specs=[pl.BlockSpec((nb, ts, D_MODEL), lambda b, s: (b, s, 0)),
                  _full(g.shape), _full(w.shape), _full(cw.shape), _full(nega.shape),
                  _full(dtb.shape), _full(bn.shape)],
        out_specs=pl.BlockSpec((nb, ts, B_HEADS * B_DV), lambda b, s: (b, s, 0)),
        scratch_shapes=[pltpu.VMEM((nb, SUBLANES, 3 * B_HEADS * B_DK), F32),
                        pltpu.VMEM((nb, B_HEADS, B_DK, B_DV), F32)],
        compiler_params=_cparams(2),
        name="mixer_b",
    )(x, g, w, cw, nega, dtb, bn)


def _proj_kernel(x_ref, g_ref, w_ref, o_ref):
    h = _rms(x_ref[...], g_ref[...]).astype(BF16)
    o_ref[...] = _dot(h, w_ref[...]).astype(o_ref.dtype)


def _norm_proj(x2, g, w, *, tm, out_dtype, name):
    rows = x2.shape[0]
    n = w.shape[1]
    return pl.pallas_call(
        _proj_kernel,
        out_shape=jax.ShapeDtypeStruct((rows, n), out_dtype),
        grid=(rows // tm,),
        in_specs=[pl.BlockSpec((tm, D_MODEL), lambda i: (i, 0)), _full(g.shape), _full(w.shape)],
        out_specs=pl.BlockSpec((tm, n), lambda i: (i, 0)),
        compiler_params=_cparams(1),
        name=name,
    )(x2, g, w)


S5_LANE_GROUPS = 128 // C_GROUP
S5_HALF = S5_CHUNK // 2


def _s5_kernel(u_ref, perm_ref, kt_ref, bs_ref, cs_ref, a1_ref, a2_ref, d_ref, o_ref, nat,
               *, rows, nsteps):
    lanes = 128
    nat[...] = u_ref[...].astype(F32)
    perm = perm_ref[...]
    zb = []
    for jh in range(2):
        vcat = jnp.concatenate(
            [nat[pl.ds(S5_HALF * jh + jl, rows, stride=S5_CHUNK), :] for jl in range(S5_HALF)],
            axis=-1)
        zb.append(_dot(vcat.astype(BF16), perm).astype(BF16))
    yb = ([], [])
    for gl in range(S5_LANE_GROUPS):
        sl = slice(gl * lanes, (gl + 1) * lanes)
        z = jnp.concatenate([zb[0][:, sl], zb[1][:, sl]], axis=-1)
        y = _dot(z, kt_ref[gl])
        hst = _dot(z, bs_ref[gl])
        for kk in range(nsteps):
            d = 1 << kk
            if d >= rows:
                break
            sh = _shift_rows(hst, d, 0.0)
            hst = (hst + a1_ref[gl, kk:kk + 1, :] * sh
                   + a2_ref[gl, kk:kk + 1, :] * pltpu.roll(sh, C_STATE, axis=1))
        hprev = _shift_rows(hst, 1, 0.0)
        y = y + _dot(hprev.astype(BF16), cs_ref[gl])
        y = y + d_ref[gl] * z.astype(F32)
        yg = _gelu(y).astype(BF16)
        yb[0].append(yg[:, :lanes])
        yb[1].append(yg[:, lanes:])
    for jh in range(2):
        ynat = _dot(jnp.concatenate(yb[jh], axis=-1), perm)
        for jl in range(S5_HALF):
            nat[pl.ds(S5_HALF * jh + jl, rows, stride=S5_CHUNK), :] = ynat[:, jl * lanes:(jl + 1) * lanes]
    o_ref[...] = nat[...].astype(o_ref.dtype)


def _s5_perm():
    import numpy as np
    n = S5_HALF * S5_LANE_GROUPS * C_GROUP
    idx = np.arange(n).reshape(S5_HALF, S5_LANE_GROUPS, C_GROUP)
    p = np.zeros((n, n), np.float32)
    p[idx.reshape(-1), idx.transpose(1, 0, 2).reshape(-1)] = 1.0
    return jnp.asarray(p, BF16)


def _s5(u, kt, bs, cs, a1, a2, dt):
    bsz, seq, width = u.shape
    rows = seq // S5_CHUNK
    lanes = S5_CHUNK * C_GROUP
    nsteps = a1.shape[1]
    gpb = S5_LANE_GROUPS
    perm = _s5_perm()
    kern = functools.partial(_s5_kernel, rows=rows, nsteps=nsteps)
    wspec = lambda shape: pl.BlockSpec((gpb,) + shape, lambda b, q: (q, 0, 0))
    return pl.pallas_call(
        kern,
        out_shape=jax.ShapeDtypeStruct(u.shape, BF16),
        grid=(bsz, width // 128),
        in_specs=[pl.BlockSpec((None, seq, 128), lambda b, q: (b, 0, q)),
                  _full(perm.shape),
                  wspec((lanes, lanes)), wspec((lanes, 2 * C_STATE)), wspec((2 * C_STATE, lanes)),
                  wspec((nsteps, 2 * C_STATE)), wspec((nsteps, 2 * C_STATE)), wspec((1, lanes))],
        out_specs=pl.BlockSpec((None, seq, 128), lambda b, q: (b, 0, q)),
        scratch_shapes=[pltpu.VMEM((seq, 128), F32)],
        compiler_params=_cparams(2),
        name="s5_groups",
    )(u, perm, kt, bs, cs, a1, a2, dt)


def _s5_tables(lam_re, lam_im, log_dt, b_re, b_im, c_re, c_im, d, nsteps):
    g, p = lam_re.shape
    cg = C_GROUP
    lc = S5_CHUNK
    dt = jnp.exp(log_dt)[:, None]
    mag = jnp.exp(lam_re * dt)
    ar, ai = mag * jnp.cos(lam_im * dt), mag * jnp.sin(lam_im * dt)
    den = lam_re * lam_re + lam_im * lam_im
    fr = ((ar - 1.0) * lam_re + ai * lam_im) / den
    fi = (ai * lam_re - (ar - 1.0) * lam_im) / den
    bbr = fr[..., None] * b_re - fi[..., None] * b_im
    bbi = fr[..., None] * b_im + fi[..., None] * b_re
    prs, pis = [jnp.ones_like(ar)], [jnp.zeros_like(ai)]
    for _ in range(lc):
        r0, i0 = prs[-1], pis[-1]
        prs.append(r0 * ar - i0 * ai)
        pis.append(r0 * ai + i0 * ar)
    pr, pi = jnp.stack(prs), jnp.stack(pis)
    clr = c_re[None] * pr[:, :, None, :] - c_im[None] * pi[:, :, None, :]
    cli = c_re[None] * pi[:, :, None, :] + c_im[None] * pr[:, :, None, :]
    kern = (jnp.einsum('tgop,gpi->tgoi', clr, bbr, precision=HIGHEST)
            - jnp.einsum('tgop,gpi->tgoi', cli, bbi, precision=HIGHEST))
    jj = jnp.arange(lc)
    tau = jj[None, :] - jj[:, None]
    sel = kern[jnp.clip(tau, 0, lc)]
    sel = jnp.where((tau >= 0)[:, :, None, None, None], sel, 0.0)
    kt = sel.transpose(2, 0, 4, 1, 3).reshape(g, lc * cg, lc * cg)
    pw_r, pw_i = pr[lc - 1 - jj], pi[lc - 1 - jj]
    sb_r = pw_r[..., None] * bbr[None] - pw_i[..., None] * bbi[None]
    sb_i = pw_r[..., None] * bbi[None] + pw_i[..., None] * bbr[None]
    bs = jnp.concatenate([sb_r.transpose(1, 0, 3, 2).reshape(g, lc * cg, p),
                          sb_i.transpose(1, 0, 3, 2).reshape(g, lc * cg, p)], axis=-1)
    co_r, co_i = clr[1:lc + 1], cli[1:lc + 1]
    cs = jnp.concatenate([co_r.transpose(1, 3, 0, 2).reshape(g, p, lc * cg),
                          -co_i.transpose(1, 3, 0, 2).reshape(g, p, lc * cg)], axis=1)
    sr, si = [pr[lc]], [pi[lc]]
    for _ in range(nsteps - 1):
        r0, i0 = sr[-1], si[-1]
        sr.append(r0 * r0 - i0 * i0)
        si.append(2.0 * r0 * i0)
    sr, si = jnp.stack(sr, axis=1), jnp.stack(si, axis=1)
    a1 = jnp.concatenate([sr, sr], axis=-1)
    a2 = jnp.concatenate([-si, si], axis=-1)
    dtile = jnp.tile(d.reshape(g, 1, cg), (1, lc, 1)).reshape(g, 1, lc * cg)
    return kt.astype(BF16), bs.astype(BF16), cs.astype(BF16), a1, a2, dtile


def _merge_kernel(x_ref, ya_ref, yb_ref, yc_ref, g_ref, wg_ref, bg_ref, glu_w_ref, glu_b_ref,
                  wbr_ref, wo_ref, o_ref):
    x = x_ref[...]
    h = _rms(x, g_ref[...]).astype(BF16)
    ycg = yc_ref[...]
    yc = (ycg.astype(F32) * jax.nn.sigmoid(_dot(ycg, glu_w_ref[...]) + glu_b_ref[...])).astype(BF16)
    m = None
    for kk, y in enumerate((ya_ref[...], yb_ref[...], yc)):
        lo = kk * D_MODEL
        gate = jax.nn.sigmoid(_dot(h, wg_ref[:, lo:lo + D_MODEL]) + bg_ref[:, lo:lo + D_MODEL])
        term = gate * _dot(y, wbr_ref[kk])
        m = term if m is None else m + term
    o_ref[...] = x + _dot(m.astype(BF16), wo_ref[...])


def _merge(x2, ya, yb, yc, g, wg, bg, glu_w, glu_b, wbr, wo, *, tm):
    rows = x2.shape[0]
    tok = lambda n: pl.BlockSpec((tm, n), lambda i: (i, 0))
    return pl.pallas_call(
        _merge_kernel,
        out_shape=jax.ShapeDtypeStruct(x2.shape, F32),
        grid=(rows // tm,),
        in_specs=[tok(D_MODEL), tok(BRANCH_WIDTH), tok(BRANCH_WIDTH), tok(BRANCH_WIDTH),
                  _full(g.shape), _full(wg.shape), _full(bg.shape), _full(glu_w.shape),
                  _full(glu_b.shape), _full(wbr.shape), _full(wo.shape)],
        out_specs=tok(D_MODEL),
        compiler_params=_cparams(1),
        name="merge",
    )(x2, ya, yb, yc, g, wg, bg, glu_w, glu_b, wbr, wo)


def _xattn_kernel(x_ref, kv_ref, g_ref, wq_ref, wo_ref, o_ref):
    x = x_ref[...]
    h = _rms(x, g_ref[...]).astype(BF16)
    q = _dot(h, wq_ref[...])
    heads = []
    for hd in range(X_HEADS):
        lo = hd * X_HEAD_DIM
        qh = q[:, lo:lo + X_HEAD_DIM].astype(BF16)
        kh = kv_ref[:, lo:lo + X_HEAD_DIM]
        vh = kv_ref[:, D_MODEL + lo:D_MODEL + lo + X_HEAD_DIM]
        sc = _dot_nt(qh, kh) * (X_HEAD_DIM ** -0.5)
        sc = sc - jnp.max(sc, axis=-1, keepdims=True)
        e = jnp.exp(sc)
        pr = e * (1.0 / jnp.sum(e, axis=-1, keepdims=True))
        heads.append(_dot(pr.astype(BF16), vh))
    o = jnp.concatenate(heads, axis=-1).astype(BF16)
    o_ref[...] = x + _dot(o, wo_ref[...])


def _xattn(x, kv, g, wq, wo, *, ts):
    bsz, seq, _ = x.shape
    mem_len = kv.shape[1]
    return pl.pallas_call(
        _xattn_kernel,
        out_shape=jax.ShapeDtypeStruct(x.shape, F32),
        grid=(bsz, seq // ts),
        in_specs=[pl.BlockSpec((None, ts, D_MODEL), lambda b, s: (b, s, 0)),
                  pl.BlockSpec((None, mem_len, 2 * D_MODEL), lambda b, s: (b, 0, 0)),
                  _full(g.shape), _full(wq.shape), _full(wo.shape)],
        out_specs=pl.BlockSpec((None, ts, D_MODEL), lambda b, s: (b, s, 0)),
        compiler_params=_cparams(2),
        name="xattn",
    )(x, kv, g, wq, wo)


def _ffn_kernel(x_ref, g_ref, wup_ref, cw_ref, cb_ref, wdn_ref, fg_ref, o_ref, tail,
                *, ts, final_norm):
    s = pl.program_id(1)

    @pl.when(s == 0)
    def _():
        tail[...] = jnp.zeros_like(tail)

    x = x_ref[...]
    h = _rms(x, g_ref[...]).astype(BF16)
    acts = []
    for n in range(D_FF // FF_TILE):
        parts = []
        for half in range(2):
            lo = half * D_FF + n * FF_TILE
            u = _dot(h, wup_ref[:, lo:lo + FF_TILE])
            parts.append(_causal_conv(u, tail[:, lo:lo + FF_TILE], cw_ref[:, lo:lo + FF_TILE],
                                      cb_ref[:, lo:lo + FF_TILE]))
            tail[:, lo:lo + FF_TILE] = u[ts - SUBLANES:, :]
        acts.append((_gelu(parts[0]) * parts[1]).astype(BF16))
    acc = x + _dot(jnp.concatenate(acts, axis=-1), wdn_ref[...])
    if final_norm:
        acc = _rms(acc, fg_ref[...])
    o_ref[...] = acc


def _ffn(x, g, wup, cw, cb, wdn, fg, *, ts, final_norm):
    bsz, seq, _ = x.shape
    kern = functools.partial(_ffn_kernel, ts=ts, final_norm=final_norm)
    return pl.pallas_call(
        kern,
        out_shape=jax.ShapeDtypeStruct(x.shape, F32),
        grid=(bsz, seq // ts),
        in_specs=[pl.BlockSpec((None, ts, D_MODEL), lambda b, s: (b, s, 0)),
                  _full(g.shape), _full(wup.shape), _full(cw.shape), _full(cb.shape),
                  _full(wdn.shape), _full(fg.shape)],
        out_specs=pl.BlockSpec((None, ts, D_MODEL), lambda b, s: (b, s, 0)),
        scratch_shapes=[pltpu.VMEM((SUBLANES, 2 * D_FF), F32)],
        compiler_params=_cparams(2),
        name="conv_ffn",
    )(x, g, wup, cw, cb, wdn, fg)


def _tile(n, pref):
    t = min(n, pref)
    while n % t:
        t //= 2
    return t


def _block_diag(w):
    hh, dd, _ = w.shape
    eye = jnp.eye(hh, dtype=w.dtype)
    return (eye[:, None, :, None] * w[:, :, None, :]).reshape(hh * dd, hh * dd)


def _row(v):
    return v.reshape(1, -1).astype(F32)


def _layer(x, mem, p, *, last, final_gain):
    bsz, seq, _ = x.shape
    tokens = bsz * seq
    w_in = p["w_in"]
    offs = [0, 512, 1024, 1536, 2048, 2560, 3072, 3076, 3080, 3592, 6664]
    col = lambda i: w_in[:, offs[i]:offs[i + 1]]
    g_mix = _row(p["mix_norm"])

    w_a = jnp.concatenate([col(0), col(1)], axis=1).astype(BF16)
    w_gate = jnp.concatenate([_block_diag(p["a_w_r"]), _block_diag(p["a_w_i"])], axis=1).astype(BF16)
    b_gate_a = jnp.concatenate([p["a_b_r"], p["a_b_i"]]).reshape(1, -1)
    ya = _mixer_a(x, g_mix, w_a, p["a_conv_w"], _row(p["a_conv_b"]), w_gate, b_gate_a,
                  _row(p["a_lam"]), ts=_tile(seq, 256))

    pad = jnp.zeros((D_MODEL, 128 - 2 * B_HEADS), F32)
    w_b = jnp.concatenate([col(2), col(3), col(4), col(5), col(6), col(7), pad], axis=1).astype(BF16)
    lane_pad = lambda v: jnp.concatenate([jnp.zeros((B_HEADS,), F32), v,
                                          jnp.zeros((128 - 2 * B_HEADS,), F32)]).reshape(1, 128)
    yb = _mixer_b(x, g_mix, w_b, p["b_conv_w"], lane_pad(-jnp.exp(p["b_a_log"])),
                  lane_pad(p["b_dt_bias"]), _row(p["b_norm"]), ts=_tile(seq, 256),
                  nb=_tile(bsz, 2))

    x2 = x.reshape(tokens, D_MODEL)
    uc = _norm_proj(x2, g_mix, col(8).astype(BF16), tm=_tile(tokens, 1024), out_dtype=BF16,
                    name="proj_c")
    rows = seq // S5_CHUNK
    nsteps = max(1, int(math.ceil(math.log2(rows))))
    tabs = _s5_tables(p["c_lam_re"], p["c_lam_im"], p["c_log_dt"], p["c_b_re"], p["c_b_im"],
                      p["c_c_re"], p["c_c_im"], p["c_d"], nsteps)
    ycg = _s5(uc.reshape(bsz, seq, BRANCH_WIDTH), *tabs).reshape(tokens, BRANCH_WIDTH)

    x2 = _merge(x2, ya.reshape(tokens, -1), yb.reshape(tokens, -1), ycg, g_mix,
                col(9).astype(BF16), _row(p["b_gate"]), p["c_glu_w"].astype(BF16), _row(p["c_glu_b"]),
                p["w_branch"].astype(BF16), p["w_out"].astype(BF16), tm=_tile(tokens, 512))
    x = x2.reshape(bsz, seq, D_MODEL)

    mem_len = mem.shape[1]
    kv = _norm_proj(mem.reshape(bsz * mem_len, D_MODEL), _row(p["mem_norm"]),
                    p["xa_w_kv"].astype(BF16), tm=_tile(bsz * mem_len, 512), out_dtype=BF16,
                    name="mem_kv").reshape(bsz, mem_len, 2 * D_MODEL)
    x = _xattn(x, kv, _row(p["xa_norm"]), p["xa_w_q"].astype(BF16), p["xa_w_o"].astype(BF16),
               ts=_tile(seq, 512))

    x = _ffn(x, _row(p["ffn_norm"]), p["ffn_w_up"].astype(BF16), p["ffn_conv_w"],
             _row(p["ffn_conv_b"]), p["ffn_w_down"].astype(BF16), _row(final_gain),
             ts=_tile(seq, 512), final_norm=last)
    return x


_LAYER_KEYS = ("mix_norm", "w_in", "b_gate", "a_conv_w", "a_conv_b", "a_w_r", "a_b_r", "a_w_i", "a_b_i",
               "a_lam", "b_conv_w", "b_a_log", "b_dt_bias", "b_norm", "c_lam_re", "c_lam_im", "c_log_dt",
               "c_b_re", "c_b_im", "c_c_re", "c_c_im", "c_d", "c_glu_w", "c_glu_b", "w_branch", "w_out",
               "xa_norm", "mem_norm", "xa_w_q", "xa_w_kv", "xa_w_o", "ffn_norm", "ffn_w_up", "ffn_conv_w",
               "ffn_conv_b", "ffn_w_down")


@jax.jit
def kernel(x, mem, mix_norm, w_in, b_gate, a_conv_w, a_conv_b, a_w_r, a_b_r, a_w_i, a_b_i, a_lam,
           b_conv_w, b_a_log, b_dt_bias, b_norm, c_lam_re, c_lam_im, c_log_dt, c_b_re, c_b_im,
           c_c_re, c_c_im, c_d, c_glu_w, c_glu_b, w_branch, w_out, xa_norm, mem_norm, xa_w_q,
           xa_w_kv, xa_w_o, ffn_norm, ffn_w_up, ffn_conv_w, ffn_conv_b, ffn_w_down, final_norm):
    stacked = dict(zip(_LAYER_KEYS, (
        mix_norm, w_in, b_gate, a_conv_w, a_conv_b, a_w_r, a_b_r, a_w_i, a_b_i, a_lam,
        b_conv_w, b_a_log, b_dt_bias, b_norm, c_lam_re, c_lam_im, c_log_dt, c_b_re, c_b_im,
        c_c_re, c_c_im, c_d, c_glu_w, c_glu_b, w_branch, w_out, xa_norm, mem_norm, xa_w_q,
        xa_w_kv, xa_w_o, ffn_norm, ffn_w_up, ffn_conv_w, ffn_conv_b, ffn_w_down)))
    depth = w_in.shape[0]
    for l in range(depth):
        p = {k: v[l] for k, v in stacked.items()}
        x = _layer(x, mem, p, last=(l == depth - 1), final_gain=final_norm)
    return x
```

```python
import functools
import math

import jax
import jax.numpy as jnp
from jax import lax
from jax.experimental import pallas as pl
from jax.experimental.pallas import tpu as pltpu

F32 = jnp.float32
BF16 = jnp.bfloat16

EPS = 1e-6
D_MODEL = 1024
BRANCH_WIDTH = 512
N_BRANCH = 3
A_HEADS = 8
A_HEAD_DIM = BRANCH_WIDTH // A_HEADS
A_CONV = 4
RG_C = 8.0
B_HEADS = 4
B_DK = 128
B_DV = 128
B_CONV = 4
DELTA_CHUNK = 64
SOLVE_REFINE_STEPS = 1
C_GROUP = 16
C_GROUPS = BRANCH_WIDTH // C_GROUP
C_STATE = 64
S5_CHUNK = 16
X_HEADS = 4
X_HEAD_DIM = D_MODEL // X_HEADS
D_FF = 3 * D_MODEL
FFN_CONV = 3
FF_TILE = 512

SUBLANES = 8
VMEM_LIMIT = 56 * 1024 * 1024

HIGHEST = lax.Precision.HIGHEST


def _cparams(n_axes):
    return pltpu.CompilerParams(dimension_semantics=("arbitrary",) * n_axes,
                                vmem_limit_bytes=VMEM_LIMIT)


def _full(shape):
    n = len(shape)
    return pl.BlockSpec(shape, lambda *_: (0,) * n)


def _rms(x, g):
    var = jnp.mean(x * x, axis=-1, keepdims=True)
    return x * lax.rsqrt(var + EPS) * g


_GELU_C = -2.0 * math.sqrt(2.0 / math.pi) * math.log2(math.e)


def _gelu(x):
    t = (x * x) * (0.044715 * _GELU_C) + _GELU_C
    return x / (1.0 + jnp.exp2(t * x))


def _softplus(x):
    return jnp.maximum(x, 0.0) + jnp.log1p(jnp.exp(-jnp.abs(x)))


def _dot(a, b):
    return jnp.dot(a, b, preferred_element_type=F32)


def _dot_nt(a, b):
    return lax.dot_general(a, b, (((1,), (1,)), ((), ())), preferred_element_type=F32)


def _dot3(a, b):
    ah = a.astype(BF16)
    al = (a - ah.astype(F32)).astype(BF16)
    bh = b.astype(BF16)
    bl = (b - bh.astype(F32)).astype(BF16)
    return _dot(ah, bh) + _dot(ah, bl) + _dot(al, bh)


def _causal_conv(u, prev, w, bias):
    taps = w.shape[0]
    rows, width = u.shape
    acc = w[taps - 1:taps, :] * u
    if bias is not None:
        acc = acc + bias
    u3 = u.reshape(rows // SUBLANES, SUBLANES, width)
    sub = lax.broadcasted_iota(jnp.int32, u3.shape, 1)
    for i in range(taps - 1):
        d = taps - 1 - i
        rot = pltpu.roll(u3, d, axis=1)
        rot_prev = jnp.concatenate([pltpu.roll(prev, d, axis=0)[None], rot[:-1]], axis=0)
        shifted = jnp.where(sub < d, rot_prev, rot).reshape(rows, width)
        acc = acc + w[i:i + 1, :] * shifted
    return acc


def _shift_rows(x, d, fill):
    row = lax.broadcasted_iota(jnp.int32, x.shape, 0)
    return jnp.where(row >= d, pltpu.roll(x, d, axis=0), fill)


def _mixer_a_kernel(x_ref, g_ref, w_ref, cw_ref, cb_ref, wg_ref, bg_ref, lam_ref, o_ref, u_ref,
                    pbuf, hstate, *, ts):
    s = pl.program_id(1)

    @pl.when(s == 0)
    def _():
        pbuf[...] = jnp.zeros_like(pbuf)
        hstate[...] = jnp.zeros_like(hstate)

    h = _rms(x_ref[...], g_ref[...]).astype(BF16)
    p = _dot(h, w_ref[...])
    xa = p[:, :BRANCH_WIDTH]
    ga = p[:, BRANCH_WIDTH:2 * BRANCH_WIDTH]
    u_ref[...] = p[:, 2 * BRANCH_WIDTH:].astype(u_ref.dtype)
    xc = _causal_conv(xa, pbuf[...], cw_ref[...], cb_ref[...])
    pbuf[...] = xa[ts - SUBLANES:, :]

    rg = _dot(xc.astype(BF16), wg_ref[...]) + bg_ref[...]
    r = jax.nn.sigmoid(rg[:, :BRANCH_WIDTH])
    ig = jax.nn.sigmoid(rg[:, BRANCH_WIDTH:])
    log_a = (-RG_C) * r * _softplus(-lam_ref[...])
    a = jnp.exp(log_a)
    om = 1.0 - a * a
    mult = jnp.where(om > 0.0, om * lax.rsqrt(om), 0.0)
    row = lax.broadcasted_iota(jnp.int32, (ts, BRANCH_WIDTH), 0)
    mult = jnp.where(jnp.logical_and(row == 0, s == 0), 1.0, mult)
    b = mult * ig * xc

    a3 = a.reshape(ts // SUBLANES, SUBLANES, BRANCH_WIDTH)
    b3 = b.reshape(ts // SUBLANES, SUBLANES, BRANCH_WIDTH)
    pos = lax.broadcasted_iota(jnp.int32, a3.shape, 1)
    d = 1
    while d < SUBLANES:
        inside = pos >= d
        b3 = a3 * jnp.where(inside, pltpu.roll(b3, d, axis=1), 0.0) + b3
        a3 = a3 * jnp.where(inside, pltpu.roll(a3, d, axis=1), 1.0)
        d *= 2
    carry = hstate[...]
    pieces = []
    for i in range(ts // SUBLANES):
        hs_i = b3[i] + a3[i] * carry
        pieces.append(hs_i)
        carry = hs_i[SUBLANES - 1:SUBLANES, :]
    hstate[...] = carry
    hs = jnp.concatenate(pieces, axis=0)
    o_ref[...] = (hs * _gelu(ga)).astype(o_ref.dtype)


def _mixer_a(x, g, w, cw, cb, wg, bg, lam, *, ts):
    bsz, seq, _ = x.shape
    kern = functools.partial(_mixer_a_kernel, ts=ts)
    return pl.pallas_call(
        kern,
        out_shape=(jax.ShapeDtypeStruct((bsz, seq, BRANCH_WIDTH), BF16),
                   jax.ShapeDtypeStruct((bsz, seq, BRANCH_WIDTH), BF16)),
        grid=(bsz, seq // ts),
        in_specs=[pl.BlockSpec((None, ts, D_MODEL), lambda b, s: (b, s, 0)),
                  _full(g.shape), _full(w.shape), _full(cw.shape), _full(cb.shape),
                  _full(wg.shape), _full(bg.shape), _full(lam.shape)],
        out_specs=(pl.BlockSpec((None, ts, BRANCH_WIDTH), lambda b, s: (b, s, 0)),
                   pl.BlockSpec((None, ts, BRANCH_WIDTH), lambda b, s: (b, s, 0))),
        scratch_shapes=[pltpu.VMEM((SUBLANES, BRANCH_WIDTH), F32),
                        pltpu.VMEM((1, BRANCH_WIDTH), F32)],
        compiler_params=_cparams(2),
        name="mixer_a",
    )(x, g, w, cw, cb, wg, bg, lam)


def _mixer_b_kernel(x_ref, g_ref, w_ref, cw_ref, nega_ref, dtb_ref, bn_ref, o_ref,
                    pbuf, state, *, ts, nb):
    s = pl.program_id(1)
    c = DELTA_CHUNK
    n_chunks = ts // c
    qkv_w = 3 * B_HEADS * B_DK

    @pl.when(s == 0)
    def _():
        pbuf[...] = jnp.zeros_like(pbuf)
        state[...] = jnp.zeros_like(state)

    ri = lax.broadcasted_iota(jnp.int32, (ts, ts), 0)
    ci = lax.broadcasted_iota(jnp.int32, (ts, ts), 1)
    same = (ri // c) == (ci // c)
    incl = jnp.logical_and(same, ci <= ri)
    strict = jnp.logical_and(same, ci < ri)
    eye = jnp.where(ri == ci, 1.0, 0.0).astype(F32)
    pos1 = lax.broadcasted_iota(jnp.int32, (ts, 128), 0) % c

    heads = []
    zs = []
    h_all = jnp.concatenate([_rms(x_ref[r], g_ref[...]).astype(BF16) for r in range(nb)], axis=0)
    p_all = _dot(h_all, w_ref[...])
    for r in range(nb):
        p = p_all[r * ts:(r + 1) * ts, :]
        pre = p[:, :qkv_w]
        zs.append(p[:, qkv_w:qkv_w + B_HEADS * B_DV])
        ba = p[:, qkv_w + B_HEADS * B_DV:]
        qkv = jax.nn.silu(_causal_conv(pre, pbuf[r], cw_ref[...], None))
        pbuf[r] = pre[ts - SUBLANES:, :]
        beta = jax.nn.sigmoid(ba)
        gc = nega_ref[...] * _softplus(ba + dtb_ref[...])
        d = 1
        while d < c:
            gc = gc + jnp.where(pos1 >= d, pltpu.roll(gc, d, axis=0), 0.0)
            d *= 2
        gct = gc.T
        egc = jnp.exp(gc)
        for hd in range(B_HEADS):
            q = qkv[:, hd * B_DK:(hd + 1) * B_DK]
            k = qkv[:, (B_HEADS + hd) * B_DK:(B_HEADS + hd + 1) * B_DK]
            v = qkv[:, (2 * B_HEADS) * B_DK + hd * B_DV:(2 * B_HEADS) * B_DK + (hd + 1) * B_DV]
            q = q * lax.rsqrt(jnp.sum(q * q, axis=-1, keepdims=True) + EPS) * (B_DK ** -0.5)
            k = k * lax.rsqrt(jnp.sum(k * k, axis=-1, keepdims=True) + EPS)
            beta_c = beta[:, hd:hd + 1]
            gc_c = gc[:, B_HEADS + hd:B_HEADS + hd + 1]
            gc_r = gct[B_HEADS + hd:B_HEADS + hd + 1, :]
            egc_c = egc[:, B_HEADS + hd:B_HEADS + hd + 1]
            decay = jnp.exp(jnp.where(incl, gc_c - gc_r, -jnp.inf))
            kb = k * beta_c
            k16 = k.astype(BF16)
            a_mat = jnp.where(strict, _dot_nt(kb.astype(BF16), k16) * decay, 0.0)
            qk = (_dot_nt(q.astype(BF16), k16) * decay).astype(BF16)
            g_last = [gc_c[n * c + c - 1:n * c + c, :] for n in range(n_chunks)]
            k_dec = [k[n * c:(n + 1) * c, :] * jnp.exp(g_last[n] - gc_c[n * c:(n + 1) * c, :])
                     for n in range(n_chunks)]
            heads.append(dict(a_mat=a_mat, rhs=jnp.concatenate([v * beta_c, kb * egc_c], axis=-1),
                              qk=qk, q_dec=(q * egc_c).astype(BF16), k_dec=k_dec,
                              g_tot=[jnp.exp(gl) for gl in g_last]))
    pw = [(-hv["a_mat"]).astype(BF16) for hv in heads]
    tinv = [eye + p_.astype(F32) for p_ in pw]
    span = 2
    while span < c:
        pw = [_dot(p_, p_).astype(BF16) for p_ in pw]
        tinv = [t_ + _dot(t_.astype(BF16), p_) for t_, p_ in zip(tinv, pw)]
        span *= 2
    t16 = [t_.astype(BF16) for t_ in tinv]
    sol = [_dot(t_, hv["rhs"].astype(BF16)) for t_, hv in zip(t16, heads)]
    for _ in range(SOLVE_REFINE_STEPS):
        resid = [hv["rhs"] - s_ - _dot3(hv["a_mat"], s_) for hv, s_ in zip(heads, sol)]
        sol = [s_ + _dot(t_, r_.astype(BF16)) for s_, t_, r_ in zip(sol, t16, resid)]
    for hv, s_ in zip(heads, sol):
        hv["u"] = s_[:, :B_DV]
        hv["w16"] = s_[:, B_DV:].astype(BF16)

    n_pairs = nb * B_HEADS
    st = [state[i // B_HEADS, i % B_HEADS] for i in range(n_pairs)]
    o_inter = [[] for _ in range(n_pairs)]
    v_new = [[] for _ in range(n_pairs)]
    zero_s = jnp.zeros((B_DK, B_DV), BF16)
    zero_v = jnp.zeros((c, B_DV), BF16)
    for n in range(n_chunks):
        lo = n * c
        for i in range(0, n_pairs, 2):
            h0, h1 = heads[i], heads[i + 1]
            lhs = jnp.concatenate(
                [jnp.concatenate([h0["w16"][lo:lo + c, :], h1["w16"][lo:lo + c, :]], axis=1),
                 jnp.concatenate([h0["q_dec"][lo:lo + c, :], h1["q_dec"][lo:lo + c, :]], axis=1)],
                axis=0)
            s2 = jnp.concatenate(
                [jnp.concatenate([st[i].astype(BF16), zero_s], axis=1),
                 jnp.concatenate([zero_s, st[i + 1].astype(BF16)], axis=1)], axis=0)
            ws = _dot(lhs, s2)
            vn0 = h0["u"][lo:lo + c, :] - ws[:c, :B_DV]
            vn1 = h1["u"][lo:lo + c, :] - ws[:c, B_DV:]
            o_inter[i].append(ws[c:, :B_DV])
            o_inter[i + 1].append(ws[c:, B_DV:])
            kdt = jnp.concatenate([h0["k_dec"][n], h1["k_dec"][n]], axis=0).T.astype(BF16)
            v2 = jnp.concatenate(
                [jnp.concatenate([vn0.astype(BF16), zero_v], axis=1),
                 jnp.concatenate([zero_v, vn1.astype(BF16)], axis=1)], axis=0)
            upd = _dot(kdt, v2)
            st[i] = st[i] * h0["g_tot"][n] + upd[:, :B_DV]
            st[i + 1] = st[i + 1] * h1["g_tot"][n] + upd[:, B_DV:]
            v_new[i].append(vn0)
            v_new[i + 1].append(vn1)

    for r in range(nb):
        outs = []
        for hd in range(B_HEADS):
            i = r * B_HEADS + hd
            state[r, hd] = st[i]
            v_all = jnp.concatenate(v_new[i], axis=0).astype(BF16)
            o = jnp.concatenate(o_inter[i], axis=0) + _dot(heads[i]["qk"], v_all)
            o = _rms(o, bn_ref[...])
            outs.append(o * jax.nn.silu(zs[r][:, hd * B_DV:(hd + 1) * B_DV]))
        o_ref[r] = jnp.concatenate(outs, axis=-1).astype(o_ref.dtype)


def _mixer_b(x, g, w, cw, nega, dtb, bn, *, ts, nb):
    bsz, seq, _ = x.shape
    kern = functools.partial(_mixer_b_kernel, ts=ts, nb=nb)
    return pl.pallas_call(
        kern,
        out_shape=jax.ShapeDtypeStruct((bsz, seq, B_HEADS * B_DV), BF16),
        grid=(bsz // nb, seq // ts),
        in_specs=[pl.BlockSpec((nb, ts, D_MODEL), lambda b, s: (b, s, 0)),
                  _full(g.shape), _full(w.shape), _full(cw.shape), _full(nega.shape),
                  _full(dtb.shape), _full(bn.shape)],
        out_specs=pl.BlockSpec((nb, ts, B_HEADS * B_DV), lambda b, s: (b, s, 0)),
        scratch_shapes=[pltpu.VMEM((nb, SUBLANES, 3 * B_HEADS * B_DK), F32),
                        pltpu.VMEM((nb, B_HEADS, B_DK, B_DV), F32)],
        compiler_params=_cparams(2),
        name="mixer_b",
    )(x, g, w, cw, nega, dtb, bn)


def _proj_kernel(x_ref, g_ref, w_ref, o_ref):
    h = _rms(x_ref[...], g_ref[...]).astype(BF16)
    o_ref[...] = _dot(h, w_ref[...]).astype(o_ref.dtype)


def _norm_proj(x2, g, w, *, tm, out_dtype, name):
    rows = x2.shape[0]
    n = w.shape[1]
    return pl.pallas_call(
        _proj_kernel,
        out_shape=jax.ShapeDtypeStruct((rows, n), out_dtype),
        grid=(rows // tm,),
        in_specs=[pl.BlockSpec((tm, D_MODEL), lambda i: (i, 0)), _full(g.shape), _full(w.shape)],
        out_specs=pl.BlockSpec((tm, n), lambda i: (i, 0)),
        compiler_params=_cparams(1),
        name=name,
    )(x2, g, w)


S5_LANE_GROUPS = 128 // C_GROUP
S5_HALF = S5_CHUNK // 2


def _s5_kernel(u_ref, perm_ref, kt_ref, bs_ref, cs_ref, a1_ref, a2_ref, d_ref, o_ref, nat,
               *, rows, nsteps):
    lanes = 128
    nat[...] = u_ref[...].astype(F32)
    perm = perm_ref[...]
    zb = []
    for jh in range(2):
        vcat = jnp.concatenate(
            [nat[pl.ds(S5_HALF * jh + jl, rows, stride=S5_CHUNK), :] for jl in range(S5_HALF)],
            axis=-1)
        zb.append(_dot(vcat.astype(BF16), perm).astype(BF16))
    yb = ([], [])
    for gl in range(S5_LANE_GROUPS):
        sl = slice(gl * lanes, (gl + 1) * lanes)
        z = jnp.concatenate([zb[0][:, sl], zb[1][:, sl]], axis=-1)
        y = _dot(z, kt_ref[gl])
        hst = _dot(z, bs_ref[gl])
        for kk in range(nsteps):
            d = 1 << kk
            if d >= rows:
                break
            sh = _shift_rows(hst, d, 0.0)
            hst = (hst + a1_ref[gl, kk:kk + 1, :] * sh
                   + a2_ref[gl, kk:kk + 1, :] * pltpu.roll(sh, C_STATE, axis=1))
        hprev = _shift_rows(hst, 1, 0.0)
        y = y + _dot(hprev.astype(BF16), cs_ref[gl])
        y = y + d_ref[gl] * z.astype(F32)
        yg = _gelu(y).astype(BF16)
        yb[0].append(yg[:, :lanes])
        yb[1].append(yg[:, lanes:])
    for jh in range(2):
        ynat = _dot(jnp.concatenate(yb[jh], axis=-1), perm)
        for jl in range(S5_HALF):
            nat[pl.ds(S5_HALF * jh + jl, rows, stride=S5_CHUNK), :] = ynat[:, jl * lanes:(jl + 1) * lanes]
    o_ref[...] = nat[...].astype(o_ref.dtype)


def _s5_perm():
    import numpy as np
    n = S5_HALF * S5_LANE_GROUPS * C_GROUP
    idx = np.arange(n).reshape(S5_HALF, S5_LANE_GROUPS, C_GROUP)
    p = np.zeros((n, n), np.float32)
    p[idx.reshape(-1), idx.transpose(1, 0, 2).reshape(-1)] = 1.0
    return jnp.asarray(p, BF16)


def _s5(u, kt, bs, cs, a1, a2, dt):
    bsz, seq, width = u.shape
    rows = seq // S5_CHUNK
    lanes = S5_CHUNK * C_GROUP
    nsteps = a1.shape[1]
    gpb = S5_LANE_GROUPS
    perm = _s5_perm()
    kern = functools.partial(_s5_kernel, rows=rows, nsteps=nsteps)
    wspec = lambda shape: pl.BlockSpec((gpb,) + shape, lambda b, q: (q, 0, 0))
    return pl.pallas_call(
        kern,
        out_shape=jax.ShapeDtypeStruct(u.shape, BF16),
        grid=(bsz, width // 128),
        in_specs=[pl.BlockSpec((None, seq, 128), lambda b, q: (b, 0, q)),
                  _full(perm.shape),
                  wspec((lanes, lanes)), wspec((lanes, 2 * C_STATE)), wspec((2 * C_STATE, lanes)),
                  wspec((nsteps, 2 * C_STATE)), wspec((nsteps, 2 * C_STATE)), wspec((1, lanes))],
        out_specs=pl.BlockSpec((None, seq, 128), lambda b, q: (b, 0, q)),
        scratch_shapes=[pltpu.VMEM((seq, 128), F32)],
        compiler_params=_cparams(2),
        name="s5_groups",
    )(u, perm, kt, bs, cs, a1, a2, dt)


def _s5_tables(lam_re, lam_im, log_dt, b_re, b_im, c_re, c_im, d, nsteps):
    g, p = lam_re.shape
    cg = C_GROUP
    lc = S5_CHUNK
    dt = jnp.exp(log_dt)[:, None]
    mag = jnp.exp(lam_re * dt)
    ar, ai = mag * jnp.cos(lam_im * dt), mag * jnp.sin(lam_im * dt)
    den = lam_re * lam_re + lam_im * lam_im
    fr = ((ar - 1.0) * lam_re + ai * lam_im) / den
    fi = (ai * lam_re - (ar - 1.0) * lam_im) / den
    bbr = fr[..., None] * b_re - fi[..., None] * b_im
    bbi = fr[..., None] * b_im + fi[..., None] * b_re
    prs, pis = [jnp.ones_like(ar)], [jnp.zeros_like(ai)]
    for _ in range(lc):
        r0, i0 = prs[-1], pis[-1]
        prs.append(r0 * ar - i0 * ai)
        pis.append(r0 * ai + i0 * ar)
    pr, pi = jnp.stack(prs), jnp.stack(pis)
    clr = c_re[None] * pr[:, :, None, :] - c_im[None] * pi[:, :, None, :]
    cli = c_re[None] * pi[:, :, None, :] + c_im[None] * pr[:, :, None, :]
    kern = (jnp.einsum('tgop,gpi->tgoi', clr, bbr, precision=HIGHEST)
            - jnp.einsum('tgop,gpi->tgoi', cli, bbi, precision=HIGHEST))
    jj = jnp.arange(lc)
    tau = jj[None, :] - jj[:, None]
    sel = kern[jnp.clip(tau, 0, lc)]
    sel = jnp.where((tau >= 0)[:, :, None, None, None], sel, 0.0)
    kt = sel.transpose(2, 0, 4, 1, 3).reshape(g, lc * cg, lc * cg)
    pw_r, pw_i = pr[lc - 1 - jj], pi[lc - 1 - jj]
    sb_r = pw_r[..., None] * bbr[None] - pw_i[..., None] * bbi[None]
    sb_i = pw_r[..., None] * bbi[None] + pw_i[..., None] * bbr[None]
    bs = jnp.concatenate([sb_r.transpose(1, 0, 3, 2).reshape(g, lc * cg, p),
                          sb_i.transpose(1, 0, 3, 2).reshape(g, lc * cg, p)], axis=-1)
    co_r, co_i = clr[1:lc + 1], cli[1:lc + 1]
    cs = jnp.concatenate([co_r.transpose(1, 3, 0, 2).reshape(g, p, lc * cg),
                          -co_i.transpose(1, 3, 0, 2).reshape(g, p, lc * cg)], axis=1)
    sr, si = [pr[lc]], [pi[lc]]
    for _ in range(nsteps - 1):
        r0, i0 = sr[-1], si[-1]
        sr.append(r0 * r0 - i0 * i0)
        si.append(2.0 * r0 * i0)
    sr, si = jnp.stack(sr, axis=1), jnp.stack(si, axis=1)
    a1 = jnp.concatenate([sr, sr], axis=-1)
    a2 = jnp.concatenate([-si, si], axis=-1)
    dtile = jnp.tile(d.reshape(g, 1, cg), (1, lc, 1)).reshape(g, 1, lc * cg)
    return kt.astype(BF16), bs.astype(BF16), cs.astype(BF16), a1, a2, dtile


def _merge_xattn_kernel(x_ref, ya_ref, yb_ref, yc_ref, kv_ref, g_ref, wg_ref, bg_ref, glu_w_ref,
                        glu_b_ref, wbr_ref, wo_ref, xg_ref, wq_ref, xwo_ref, o_ref):
    x = x_ref[...]
    h = _rms(x, g_ref[...]).astype(BF16)
    ycg = yc_ref[...]
    yc = (ycg.astype(F32) * jax.nn.sigmoid(_dot(ycg, glu_w_ref[...]) + glu_b_ref[...])).astype(BF16)
    m = None
    for kk, y in enumerate((ya_ref[...], yb_ref[...], yc)):
        lo = kk * D_MODEL
        gate = jax.nn.sigmoid(_dot(h, wg_ref[:, lo:lo + D_MODEL]) + bg_ref[:, lo:lo + D_MODEL])
        term = gate * _dot(y, wbr_ref[kk])
        m = term if m is None else m + term
    x = x + _dot(m.astype(BF16), wo_ref[...])

    h = _rms(x, xg_ref[...]).astype(BF16)
    q = _dot(h, wq_ref[...])
    heads = []
    for hd in range(X_HEADS):
        lo = hd * X_HEAD_DIM
        qh = q[:, lo:lo + X_HEAD_DIM].astype(BF16)
        kh = kv_ref[:, lo:lo + X_HEAD_DIM]
        vh = kv_ref[:, D_MODEL + lo:D_MODEL + lo + X_HEAD_DIM]
        sc = _dot_nt(qh, kh) * (X_HEAD_DIM ** -0.5)
        sc = sc - jnp.max(sc, axis=-1, keepdims=True)
        e = jnp.exp(sc)
        pr = e * (1.0 / jnp.sum(e, axis=-1, keepdims=True))
        heads.append(_dot(pr.astype(BF16), vh))
    o = jnp.concatenate(heads, axis=-1).astype(BF16)
    o_ref[...] = x + _dot(o, xwo_ref[...])


def _merge_xattn(x, ya, yb, yc, kv, g, wg, bg, glu_w, glu_b, wbr, wo, xg, wq, xwo, *, ts):
    bsz, seq, _ = x.shape
    mem_len = kv.shape[1]
    tok = lambda n: pl.BlockSpec((None, ts, n), lambda b, s: (b, s, 0))
    consts = (g, wg, bg, glu_w, glu_b, wbr, wo, xg, wq, xwo)
    return pl.pallas_call(
        _merge_xattn_kernel,
        out_shape=jax.ShapeDtypeStruct(x.shape, F32),
        grid=(bsz, seq // ts),
        in_specs=[tok(D_MODEL), tok(BRANCH_WIDTH), tok(BRANCH_WIDTH), tok(BRANCH_WIDTH),
                  pl.BlockSpec((None, mem_len, 2 * D_MODEL), lambda b, s: (b, 0, 0))]
                 + [_full(a.shape) for a in consts],
        out_specs=tok(D_MODEL),
        compiler_params=_cparams(2),
        name="merge_xattn",
    )(x, ya, yb, yc, kv, *consts)


def _ffn_kernel(x_ref, g_ref, wup_ref, cw_ref, cb_ref, wdn_ref, fg_ref, o_ref, tail,
                *, ts, final_norm):
    s = pl.program_id(1)

    @pl.when(s == 0)
    def _():
        tail[...] = jnp.zeros_like(tail)

    x = x_ref[...]
    h = _rms(x, g_ref[...]).astype(BF16)
    acts = []
    for n in range(D_FF // FF_TILE):
        parts = []
        for half in range(2):
            lo = half * D_FF + n * FF_TILE
            u = _dot(h, wup_ref[:, lo:lo + FF_TILE])
            parts.append(_causal_conv(u, tail[:, lo:lo + FF_TILE], cw_ref[:, lo:lo + FF_TILE],
                                      cb_ref[:, lo:lo + FF_TILE]))
            tail[:, lo:lo + FF_TILE] = u[ts - SUBLANES:, :]
        acts.append((_gelu(parts[0]) * parts[1]).astype(BF16))
    acc = x + _dot(jnp.concatenate(acts, axis=-1), wdn_ref[...])
    if final_norm:
        acc = _rms(acc, fg_ref[...])
    o_ref[...] = acc


def _ffn(x, g, wup, cw, cb, wdn, fg, *, ts, final_norm):
    bsz, seq, _ = x.shape
    kern = functools.partial(_ffn_kernel, ts=ts, final_norm=final_norm)
    return pl.pallas_call(
        kern,
        out_shape=jax.ShapeDtypeStruct(x.shape, F32),
        grid=(bsz, seq // ts),
        in_specs=[pl.BlockSpec((None, ts, D_MODEL), lambda b, s: (b, s, 0)),
                  _full(g.shape), _full(wup.shape), _full(cw.shape), _full(cb.shape),
                  _full(wdn.shape), _full(fg.shape)],
        out_specs=pl.BlockSpec((None, ts, D_MODEL), lambda b, s: (b, s, 0)),
        scratch_shapes=[pltpu.VMEM((SUBLANES, 2 * D_FF), F32)],
        compiler_params=_cparams(2),
        name="conv_ffn",
    )(x, g, wup, cw, cb, wdn, fg)


def _tile(n, pref):
    t = min(n, pref)
    while n % t:
        t //= 2
    return t


def _block_diag(w):
    hh, dd, _ = w.shape
    eye = jnp.eye(hh, dtype=w.dtype)
    return (eye[:, None, :, None] * w[:, :, None, :]).reshape(hh * dd, hh * dd)


def _row(v):
    return v.reshape(1, -1).astype(F32)


def _layer(x, mem, p, *, last, final_gain):
    bsz, seq, _ = x.shape
    w_in = p["w_in"]
    offs = [0, 512, 1024, 1536, 2048, 2560, 3072, 3076, 3080, 3592, 6664]
    col = lambda i: w_in[:, offs[i]:offs[i + 1]]
    g_mix = _row(p["mix_norm"])

    w_a = jnp.concatenate([col(0), col(1), col(8)], axis=1).astype(BF16)
    w_gate = jnp.concatenate([_block_diag(p["a_w_r"]), _block_diag(p["a_w_i"])], axis=1).astype(BF16)
    b_gate_a = jnp.concatenate([p["a_b_r"], p["a_b_i"]]).reshape(1, -1)
    ya, uc = _mixer_a(x, g_mix, w_a, p["a_conv_w"], _row(p["a_conv_b"]), w_gate, b_gate_a,
                      _row(p["a_lam"]), ts=_tile(seq, 256))

    pad = jnp.zeros((D_MODEL, 128 - 2 * B_HEADS), F32)
    w_b = jnp.concatenate([col(2), col(3), col(4), col(5), col(6), col(7), pad], axis=1).astype(BF16)
    lane_pad = lambda v: jnp.concatenate([jnp.zeros((B_HEADS,), F32), v,
                                          jnp.zeros((128 - 2 * B_HEADS,), F32)]).reshape(1, 128)
    yb = _mixer_b(x, g_mix, w_b, p["b_conv_w"], lane_pad(-jnp.exp(p["b_a_log"])),
                  lane_pad(p["b_dt_bias"]), _row(p["b_norm"]), ts=_tile(seq, 128),
                  nb=_tile(bsz, 4))

    rows = seq // S5_CHUNK
    nsteps = max(1, int(math.ceil(math.log2(rows))))
    tabs = _s5_tables(p["c_lam_re"], p["c_lam_im"], p["c_log_dt"], p["c_b_re"], p["c_b_im"],
                      p["c_c_re"], p["c_c_im"], p["c_d"], nsteps)
    ycg = _s5(uc, *tabs)

    mem_len = mem.shape[1]
    kv = _norm_proj(mem.reshape(bsz * mem_len, D_MODEL), _row(p["mem_norm"]),
                    p["xa_w_kv"].astype(BF16), tm=_tile(bsz * mem_len, 512), out_dtype=BF16,
                    name="mem_kv").reshape(bsz, mem_len, 2 * D_MODEL)
    x = _merge_xattn(x, ya, yb, ycg, kv, g_mix,
                     col(9).astype(BF16), _row(p["b_gate"]), p["c_glu_w"].astype(BF16),
                     _row(p["c_glu_b"]), p["w_branch"].astype(BF16), p["w_out"].astype(BF16),
                     _row(p["xa_norm"]), p["xa_w_q"].astype(BF16), p["xa_w_o"].astype(BF16),
                     ts=_tile(seq, 512))

    x = _ffn(x, _row(p["ffn_norm"]), p["ffn_w_up"].astype(BF16), p["ffn_conv_w"],
             _row(p["ffn_conv_b"]), p["ffn_w_down"].astype(BF16), _row(final_gain),
             ts=_tile(seq, 512), final_norm=last)
    return x


_LAYER_KEYS = ("mix_norm", "w_in", "b_gate", "a_conv_w", "a_conv_b", "a_w_r", "a_b_r", "a_w_i", "a_b_i",
               "a_lam", "b_conv_w", "b_a_log", "b_dt_bias", "b_norm", "c_lam_re", "c_lam_im", "c_log_dt",
               "c_b_re", "c_b_im", "c_c_re", "c_c_im", "c_d", "c_glu_w", "c_glu_b", "w_branch", "w_out",
               "xa_norm", "mem_norm", "xa_w_q", "xa_w_kv", "xa_w_o", "ffn_norm", "ffn_w_up", "ffn_conv_w",
               "ffn_conv_b", "ffn_w_down")


@jax.jit
def kernel(x, mem, mix_norm, w_in, b_gate, a_conv_w, a_conv_b, a_w_r, a_b_r, a_w_i, a_b_i, a_lam,
           b_conv_w, b_a_log, b_dt_bias, b_norm, c_lam_re, c_lam_im, c_log_dt, c_b_re, c_b_im,
           c_c_re, c_c_im, c_d, c_glu_w, c_glu_b, w_branch, w_out, xa_norm, mem_norm, xa_w_q,
           xa_w_kv, xa_w_o, ffn_norm, ffn_w_up, ffn_conv_w, ffn_conv_b, ffn_w_down, final_norm):
    stacked = dict(zip(_LAYER_KEYS, (
        mix_norm, w_in, b_gate, a_conv_w, a_conv_b, a_w_r, a_b_r, a_w_i, a_b_i, a_lam,
        b_conv_w, b_a_log, b_dt_bias, b_norm, c_lam_re, c_lam_im, c_log_dt, c_b_re, c_b_im,
        c_c_re, c_c_im, c_d, c_glu_w, c_glu_b, w_branch, w_out, xa_norm, mem_norm, xa_w_q,
        xa_w_kv, xa_w_o, ffn_norm, ffn_w_up, ffn_conv_w, ffn_conv_b, ffn_w_down)))
    depth = w_in.shape[0]
    for l in range(depth):
        p = {k: v[l] for k, v in stacked.items()}
        x = _layer(x, mem, p, last=(l == depth - 1), final_gain=final_norm)
    return x
```
